```python
import math
import jax, jax.numpy as jnp
from jax import lax
import numpy as np

D_MODEL = 1024
BATCH = 8
SEQ = 8192
DEPTH = 1
DEC_BATCH = 32
DEC_SEQ = 64
PAST_LEN = 1024

CHUNK = 64
HEAD_DIM = 64
H_A = 8
D_A = H_A * HEAD_DIM
H_IDX = 8
D_IDX = 64
TOPK_MAX = 256
QBLOCK = 128
T5_BUCKETS = 32
T5_MAX_DIST = 128
H_B = 8
D_B = H_B * HEAD_DIM
BAND_CHUNKS = 8
BAND_ROWS = BAND_CHUNKS * CHUNK
REL_CLIP = 128
D_MIX = D_A + D_B
N_GROUPS = 4
EXPERTS_PER_GROUP = 8
N_EXPERTS = N_GROUPS * EXPERTS_PER_GROUP
TOP_K_EXPERT = 2
D_EXPERT = 512
EBLOCK = 256
PROJ_SPLITS = (D_A, D_A, D_A, H_IDX * D_IDX, D_IDX, H_IDX, D_B, D_B, D_B)
N_IN = sum(PROJ_SPLITS)
DEEPNORM_ALPHA = (2 * DEPTH) ** 0.25
DEEPNORM_BETA = (8 * DEPTH) ** -0.25
LN_EPS = 1e-5
NEG_INF = -1e30
IDX_SCALE = (H_IDX * D_IDX) ** -0.5

kernel_name = 'hymba_dsa_band_hmoe_stream_step'


def _layer_norm(x, g, b):
    xf = x.astype(jnp.float32)
    mu = jnp.mean(xf, axis=-1, keepdims=True)
    var = jnp.mean(jnp.square(xf - mu), axis=-1, keepdims=True)
    return ((xf - mu) * lax.rsqrt(var + LN_EPS) * g + b).astype(x.dtype)


def _t5_bucket(rel):
    half = T5_BUCKETS // 2
    max_exact = half // 2
    ret = jnp.where(rel > 0, half, 0)
    n = jnp.abs(rel)
    nf = jnp.maximum(n, 1).astype(jnp.float32)
    large = max_exact + (jnp.log(nf / max_exact) / math.log(T5_MAX_DIST / max_exact)
                         * (half - max_exact)).astype(jnp.int32)
    large = jnp.minimum(large, half - 1)
    return ret + jnp.where(n < max_exact, n, large)


def _project(x, w_in, ln_idx_g, ln_idx_b):
    B, T, _ = x.shape
    z = jnp.einsum('btd,dn->btn', x, w_in)
    points, acc = [], 0
    for s in PROJ_SPLITS[:-1]:
        acc += s
        points.append(acc)
    qa, ka, va, qi, ki, wi, qb, kb, vb = jnp.split(z, points, axis=-1)
    heads = lambda a, h: a.reshape(B, T, h, HEAD_DIM)
    ki = _layer_norm(ki, ln_idx_g, ln_idx_b)
    return (heads(qa, H_A), heads(ka, H_A), heads(va, H_A), qi.reshape(B, T, H_IDX, D_IDX), ki, wi,
            heads(qb, H_B), heads(kb, H_B), heads(vb, H_B))


def _dsa_block(q, qi, wi, qpos, k_all, v_all, ki_all, kpos, t5_table, topk):
    adm = (kpos[None, :] // CHUNK) <= (qpos[:, None] // CHUNK)
    rel = jnp.einsum('thd,sd->ths', qi.astype(jnp.float32), ki_all.astype(jnp.float32))
    score = jnp.einsum('th,ths->ts', wi.astype(jnp.float32) * IDX_SCALE, jax.nn.relu(rel))
    score = jnp.where(adm, score, NEG_INF)
    _, idx = lax.top_k(score, topk)
    ok = jnp.take_along_axis(adm, idx, axis=1)
    kg = k_all[idx]
    vg = v_all[idx]
    logits = jnp.einsum('thd,tkhd->thk', q, kg).astype(jnp.float32) * (HEAD_DIM ** -0.5)
    bias = t5_table[_t5_bucket(kpos[idx] - qpos[:, None])]
    logits = logits + jnp.transpose(bias, (0, 2, 1)).astype(jnp.float32)
    logits = jnp.where(ok[:, None, :], logits, NEG_INF)
    p = jax.nn.softmax(logits, axis=-1).astype(v_all.dtype)
    return jnp.einsum('thk,tkhd->thd', p, vg)


def _dsa_attention(q, qi, wi, qpos, k, v, ki, kpos, t5_table, qblock):
    B, T, H, dh = q.shape
    L = k.shape[1]
    topk = min(TOPK_MAX, L // 4)
    nb = T // qblock

    def per_seq(args):
        q1, qi1, wi1, k1, v1, ki1 = args

        def per_block(bargs):
            qb, qib, wib, pb = bargs
            return _dsa_block(qb, qib, wib, pb, k1, v1, ki1, kpos, t5_table, topk)

        blocks = (q1.reshape(nb, qblock, H, dh), qi1.reshape(nb, qblock, H_IDX, D_IDX),
                  wi1.reshape(nb, qblock, H_IDX), qpos.reshape(nb, qblock))
        return lax.map(per_block, blocks).reshape(T, H, dh)

    return lax.map(per_seq, (q, qi, wi, k, v, ki))


def _band_block(q, k_band, v_band, qpos, kpos, kvalid, rel_b):
    logits = jnp.einsum('bthd,bshd->bhts', q, k_band).astype(jnp.float32) * (HEAD_DIM ** -0.5)
    d = jnp.clip(qpos[:, None] - kpos[None, :], -REL_CLIP, REL_CLIP) + REL_CLIP
    logits = logits + jnp.transpose(rel_b[d], (2, 0, 1))[None].astype(jnp.float32)
    if kvalid is not None:
        logits = jnp.where(kvalid[None, None, None, :], logits, NEG_INF)
    p = jax.nn.softmax(logits, axis=-1).astype(v_band.dtype)
    return jnp.einsum('bhts,bshd->bthd', p, v_band)


def _band_prompt(q, k, v, rel_b):
    B, S, H, dh = q.shape
    nc = S // CHUNK
    pad = ((0, 0), (BAND_ROWS, 0), (0, 0), (0, 0))
    kpad, vpad = jnp.pad(k, pad), jnp.pad(v, pad)

    def per_chunk(c):
        start = c * CHUNK
        qc = lax.dynamic_slice_in_dim(q, start, CHUNK, axis=1)
        kb = lax.dynamic_slice_in_dim(kpad, start, BAND_ROWS + CHUNK, axis=1)
        vb = lax.dynamic_slice_in_dim(vpad, start, BAND_ROWS + CHUNK, axis=1)
        qpos = start + jnp.arange(CHUNK, dtype=jnp.int32)
        kpos = start - BAND_ROWS + jnp.arange(BAND_ROWS + CHUNK, dtype=jnp.int32)
        return _band_block(qc, kb, vb, qpos, kpos, kpos >= 0, rel_b)

    out = lax.map(per_chunk, jnp.arange(nc, dtype=jnp.int32))
    return jnp.transpose(out, (1, 0, 2, 3, 4)).reshape(B, S, H, dh)


def _grouped_expert_mlp(xt, tok, eid, gate, w_gate, w_up, w_down):
    m = eid.shape[0]
    order = jnp.argsort(eid, stable=True)
    eid_s, tok_s, gate_s = eid[order], tok[order], gate[order]
    counts = jnp.zeros((N_EXPERTS,), jnp.int32).at[eid].add(1)
    start = jnp.cumsum(counts) - counts
    padded = (counts + EBLOCK - 1) // EBLOCK * EBLOCK
    pend = jnp.cumsum(padded)
    pstart = pend - padded
    dest = pstart[eid_s] + (jnp.arange(m, dtype=jnp.int32) - start[eid_s])
    n_blocks = -(-m // EBLOCK) + N_EXPERTS
    buf_tok = jnp.zeros((n_blocks * EBLOCK,), jnp.int32).at[dest].set(tok_s)
    buf_gate = jnp.zeros((n_blocks * EBLOCK,), gate.dtype).at[dest].set(gate_s)
    blk_e = jnp.minimum(jnp.searchsorted(pend, jnp.arange(n_blocks, dtype=jnp.int32) * EBLOCK, side='right'),
                        N_EXPERTS - 1)

    def run_block(args):
        tb, gb, e = args
        xb = xt[tb]
        h = jax.nn.silu(xb @ w_gate[e]) * (xb @ w_up[e])
        return (h @ w_down[e]) * gb[:, None]

    yb = lax.map(run_block, (buf_tok.reshape(n_blocks, EBLOCK), buf_gate.reshape(n_blocks, EBLOCK), blk_e))
    return jnp.zeros_like(xt).at[buf_tok].add(yb.reshape(-1, xt.shape[1]))


def _hier_moe(x, w_group, w_expert, w_gate, w_up, w_down):
    B, T, D = x.shape
    n = B * T
    xt = x.reshape(n, D)
    g_prob = jax.nn.softmax((xt @ w_group).astype(jnp.float32), axis=-1)
    g_sel = jnp.argmax(g_prob, axis=-1)
    g_gate = jnp.max(g_prob, axis=-1, keepdims=True)
    e_logits = (xt @ w_expert).astype(jnp.float32).reshape(n, N_GROUPS, EXPERTS_PER_GROUP)
    e_logits = e_logits[jnp.arange(n), g_sel]
    e_val, e_loc = lax.top_k(e_logits, TOP_K_EXPERT)
    e_gate = jax.nn.softmax(e_val, axis=-1) * g_gate
    eid = (g_sel[:, None] * EXPERTS_PER_GROUP + e_loc).reshape(-1).astype(jnp.int32)
    gate = e_gate.reshape(-1).astype(x.dtype)
    tok = jnp.repeat(jnp.arange(n, dtype=jnp.int32), TOP_K_EXPERT)
    return _grouped_expert_mlp(xt, tok, eid, gate, w_gate, w_up, w_down).reshape(B, T, D)


def _layer(x, past, t5_table, w_in, ln_idx_g, ln_idx_b, rel_b, w_o, ln1_g, ln1_b,
           w_group, w_expert, w_gate, w_up, w_down, ln2_g, ln2_b):
    B, T, _ = x.shape
    qa, ka, va, qi, ki, wi, qb, kb, vb = _project(x, w_in, ln_idx_g, ln_idx_b)
    if past is None:
        pos = jnp.arange(T, dtype=jnp.int32)
        oa = _dsa_attention(qa, qi, wi, pos, ka, va, ki, pos, t5_table, QBLOCK)
        ob = _band_prompt(qb, kb, vb, rel_b)
        keep = min(BAND_ROWS, T)
        new_b_k, new_b_v = kb[:, T - keep:], vb[:, T - keep:]
    else:
        ca_k, ca_v, ca_ki, cb_k, cb_v = past
        P, W = ca_k.shape[1], cb_k.shape[1]
        qpos = P + jnp.arange(T, dtype=jnp.int32)
        kpos = jnp.arange(P + T, dtype=jnp.int32)
        oa = _dsa_attention(qa, qi, wi, qpos,
                            jnp.concatenate([ca_k, ka], axis=1), jnp.concatenate([ca_v, va], axis=1),
                            jnp.concatenate([ca_ki, ki], axis=1), kpos, t5_table, T)
        band_k = jnp.concatenate([cb_k, kb], axis=1)
        band_v = jnp.concatenate([cb_v, vb], axis=1)
        bpos = (P - W) + jnp.arange(W + T, dtype=jnp.int32)
        ob = _band_block(qb, band_k, band_v, qpos, bpos, None, rel_b)
        new_b_k, new_b_v = band_k[:, T:], band_v[:, T:]
    mixed = jnp.concatenate([oa.reshape(B, T, D_A), ob.reshape(B, T, D_B)], axis=-1)
    h = jnp.einsum('btm,md->btd', mixed, w_o)
    x = _layer_norm(DEEPNORM_ALPHA * x + h, ln1_g, ln1_b)
    x = _layer_norm(DEEPNORM_ALPHA * x + _hier_moe(x, w_group, w_expert, w_gate, w_up, w_down), ln2_g, ln2_b)
    return x, (ka, va, ki, new_b_k, new_b_v)


def setup_inputs(seed: int = 0) -> dict:
    key = jax.random.key(seed)
    ks = jax.random.split(key, 24)
    nrm = lambda k, shape, s: jax.random.normal(k, shape, jnp.float32) * s
    w_cache_b = min(BAND_ROWS, PAST_LEN)
    return {
        'x_prompt': nrm(ks[0], (BATCH, SEQ, D_MODEL), 1.0),
        'x_sample': nrm(ks[1], (DEC_BATCH, DEC_SEQ, D_MODEL), 1.0),
        'cache_a_k': nrm(ks[2], (DEPTH, DEC_BATCH, PAST_LEN, H_A, HEAD_DIM), 1.0),
        'cache_a_v': nrm(ks[3], (DEPTH, DEC_BATCH, PAST_LEN, H_A, HEAD_DIM), 1.0),
        'cache_a_kidx': nrm(ks[4], (DEPTH, DEC_BATCH, PAST_LEN, D_IDX), 1.0),
        'cache_b_k': nrm(ks[5], (DEPTH, DEC_BATCH, w_cache_b, H_B, HEAD_DIM), 1.0),
        'cache_b_v': nrm(ks[6], (DEPTH, DEC_BATCH, w_cache_b, H_B, HEAD_DIM), 1.0),
        't5_table': nrm(ks[7], (T5_BUCKETS, H_A), 0.1),
        'w_in': nrm(ks[8], (DEPTH, D_MODEL, N_IN), D_MODEL ** -0.5),
        'ln_idx_g': 1.0 + nrm(ks[9], (DEPTH, D_IDX), 0.02),
        'ln_idx_b': nrm(ks[10], (DEPTH, D_IDX), 0.02),
        'rel_b': nrm(ks[11], (DEPTH, 2 * REL_CLIP + 1, H_B), 0.1),
        'w_o': nrm(ks[12], (DEPTH, D_MIX, D_MODEL), D_MIX ** -0.5 * DEEPNORM_BETA),
        'ln1_g': 1.0 + nrm(ks[13], (DEPTH, D_MODEL), 0.02),
        'ln1_b': nrm(ks[14], (DEPTH, D_MODEL), 0.02),
        'w_group': nrm(ks[15], (DEPTH, D_MODEL, N_GROUPS), D_MODEL ** -0.5),
        'w_expert': nrm(ks[16], (DEPTH, D_MODEL, N_EXPERTS), D_MODEL ** -0.5),
        'w_gate': nrm(ks[17], (DEPTH, N_EXPERTS, D_MODEL, D_EXPERT), D_MODEL ** -0.5),
        'w_up': nrm(ks[18], (DEPTH, N_EXPERTS, D_MODEL, D_EXPERT), D_MODEL ** -0.5),
        'w_down': nrm(ks[19], (DEPTH, N_EXPERTS, D_EXPERT, D_MODEL), D_EXPERT ** -0.5 * DEEPNORM_BETA),
        'ln2_g': 1.0 + nrm(ks[20], (DEPTH, D_MODEL), 0.02),
        'ln2_b': nrm(ks[21], (DEPTH, D_MODEL), 0.02),
    }


def reference(x_prompt, x_sample, cache_a_k, cache_a_v, cache_a_kidx, cache_b_k, cache_b_v,
              t5_table, w_in, ln_idx_g, ln_idx_b, rel_b, w_o, ln1_g, ln1_b,
              w_group, w_expert, w_gate, w_up, w_down, ln2_g, ln2_b):
    yp, ys = x_prompt, x_sample
    sp, ss = [], []
    for l in range(DEPTH):
        weights = (t5_table, w_in[l], ln_idx_g[l], ln_idx_b[l], rel_b[l], w_o[l], ln1_g[l], ln1_b[l],
                   w_group[l], w_expert[l], w_gate[l], w_up[l], w_down[l], ln2_g[l], ln2_b[l])
        yp, st_p = _layer(yp, None, *weights)
        past = (cache_a_k[l], cache_a_v[l], cache_a_kidx[l], cache_b_k[l], cache_b_v[l])
        ys, st_s = _layer(ys, past, *weights)
        sp.append(st_p)
        ss.append(st_s)
    stack = lambda states, i: jnp.stack([s[i] for s in states], axis=0)
    return (yp, ys,
            stack(sp, 0), stack(sp, 1), stack(sp, 2), stack(sp, 3), stack(sp, 4),
            stack(ss, 0), stack(ss, 1), stack(ss, 2), stack(ss, 3), stack(ss, 4))
```

```python
import functools
import math

import numpy as np
import jax
import jax.numpy as jnp
from jax import lax
from jax.experimental import pallas as pl
from jax.experimental.pallas import tpu as pltpu

D_MODEL = 1024
CHUNK = 64
HEAD_DIM = 64
N_HEADS = 8
D_HEADS = N_HEADS * HEAD_DIM
D_IDX = 64
TOPK_MAX = 256
T5_BUCKETS = 32
T5_MAX_DIST = 128
BAND_CHUNKS = 8
BAND_ROWS = BAND_CHUNKS * CHUNK
REL_CLIP = 128
N_GROUPS = 4
EXPERTS_PER_GROUP = 8
N_EXPERTS = N_GROUPS * EXPERTS_PER_GROUP
D_EXPERT = 512
DEPTH = 1
DEEPNORM_ALPHA = (2 * DEPTH) ** 0.25
LN_EPS = 1e-5
NEG_INF = -1e30
IDX_SCALE = (N_HEADS * D_IDX) ** -0.5

TILE = 256
CHUNKS_PER_TILE = TILE // CHUNK
LANES = 128
PROJ_ROWS = 512
EXPERT_ROWS = 256
V7X_VMEM_LIMIT = 56 * 1024 * 1024

_BF16 = jnp.bfloat16
_F32 = jnp.float32


def _sortable_key_np(v):
    b = np.array([v], np.float32).view(np.int32)[0]
    return np.int32(b ^ ((b >> 31) & 0x7FFFFFFF))


_SIGN = np.int32(-2 ** 31)


def _params(sem, vmem=V7X_VMEM_LIMIT):
    return pltpu.CompilerParams(dimension_semantics=sem, vmem_limit_bytes=vmem)


def _layer_norm_rows(x, g, b):
    mu = jnp.mean(x, axis=-1, keepdims=True)
    xc = x - mu
    var = jnp.mean(xc * xc, axis=-1, keepdims=True)
    return xc * lax.rsqrt(var + LN_EPS) * g + b


def _project_kernel(x_ref, wnat_ref, wsm_ref, wt_ref, wwi_ref, g_ref, b_ref,
                    ka_ref, va_ref, kb_ref, vb_ref, ki_ref,
                    kab_ref, kbb_ref, ki2_ref,
                    qat_ref, vat_ref, qit_ref, qbt_ref, vbt_ref, wit_ref):
    x = x_ref[...]
    xb = x.astype(_BF16)
    xtb = x.T.astype(_BF16)

    def nat(i):
        return jnp.dot(xb, wnat_ref[:, i * D_HEADS:(i + 1) * D_HEADS], preferred_element_type=_F32)

    ka = nat(0)
    ka_ref[...] = ka
    kab_ref[...] = ka.astype(_BF16)
    va_ref[...] = nat(1)
    kb = nat(2)
    kb_ref[...] = kb
    kbb_ref[...] = kb.astype(_BF16)
    vb_ref[...] = nat(3)

    zs = jnp.dot(xb, wsm_ref[...], preferred_element_type=_F32)
    lane = lax.broadcasted_iota(jnp.int32, zs.shape, 1)
    mu = jnp.sum(zs, axis=-1, keepdims=True) * (1.0 / D_IDX)
    zc = jnp.where(lane < D_IDX, zs - mu, 0.0)
    var = jnp.sum(zc * zc, axis=-1, keepdims=True) * (1.0 / D_IDX)
    ln = zc * lax.rsqrt(var + LN_EPS) * g_ref[...] + b_ref[...]
    ki_ref[...] = ln[:, :D_IDX]
    ki2_ref[:, :LANES] = ln.astype(_BF16)
    ki2_ref[:, LANES:] = pltpu.roll(ln, D_IDX, axis=1).astype(_BF16)

    def tr(i, scale):
        z = jnp.dot(wt_ref[i * D_HEADS:(i + 1) * D_HEADS, :], xtb, preferred_element_type=_F32)
        if scale != 1.0:
            z = z * scale
        return z.astype(_BF16)

    def put(ref, z):
        for t in range(ref.shape[0]):
            ref[t] = z[:, t * TILE:(t + 1) * TILE]

    put(qat_ref, tr(0, HEAD_DIM ** -0.5))
    put(vat_ref, tr(1, 1.0))
    put(qit_ref, tr(2, 1.0))
    put(qbt_ref, tr(3, HEAD_DIM ** -0.5))
    put(vbt_ref, tr(4, 1.0))
    wi = jnp.dot(wwi_ref[...], xtb, preferred_element_type=_F32) * IDX_SCALE
    put(wit_ref, wi[:N_HEADS])


def _project(x, wnat, wsm, wt, wwi, g, b):
    n = x.shape[0]
    rows = PROJ_ROWS
    tiles = rows // TILE
    nt = n // TILE
    grid = (n // rows,)
    row_spec = lambda w: pl.BlockSpec((rows, w), lambda i: (i, 0))
    full = lambda a: pl.BlockSpec(a.shape, lambda i: (0,) * a.ndim)
    tspec = lambda h: pl.BlockSpec((tiles, h, TILE), lambda i: (i, 0, 0))
    nat_f32 = jax.ShapeDtypeStruct((n, D_HEADS), _F32)
    nat_b16 = jax.ShapeDtypeStruct((n, D_HEADS), _BF16)
    tr_b16 = jax.ShapeDtypeStruct((nt, D_HEADS, TILE), _BF16)
    out_shape = (nat_f32, nat_f32, nat_f32, nat_f32, jax.ShapeDtypeStruct((n, D_IDX), _F32),
                 nat_b16, nat_b16, jax.ShapeDtypeStruct((n, 2 * LANES), _BF16),
                 tr_b16, tr_b16, tr_b16, tr_b16, tr_b16,
                 jax.ShapeDtypeStruct((nt, N_HEADS, TILE), _F32))
    out_specs = (row_spec(D_HEADS),) * 4 + (row_spec(D_IDX),) + (row_spec(D_HEADS),) * 2 + (row_spec(2 * LANES),) \
        + (tspec(D_HEADS),) * 5 + (tspec(N_HEADS),)
    return pl.pallas_call(
        _project_kernel, out_shape=out_shape, grid=grid,
        in_specs=[row_spec(D_MODEL), full(wnat), full(wsm), full(wt), full(wwi), full(g), full(b)],
        out_specs=out_specs, name="project",
        compiler_params=_params(("parallel",)),
    )(x, wnat, wsm, wt, wwi, g, b)


def _attend_tile(k_slab, qz, add, vt, m, l, acc):
    s = jnp.dot(k_slab, qz, preferred_element_type=_F32) + add
    m_new = jnp.maximum(m, jnp.max(s, axis=0, keepdims=True))
    alpha = jnp.exp(m - m_new)
    p = jnp.exp(s - m_new)
    l_new = alpha * l + jnp.sum(p, axis=0, keepdims=True)
    acc_new = alpha * acc + jnp.dot(vt, p.astype(_BF16), preferred_element_type=_F32)
    return m_new, l_new, acc_new


def _fill_padded_queries(qt_ref, qz_ref):
    zeros = jnp.zeros((HEAD_DIM, TILE), _BF16)
    for h in range(N_HEADS):
        lo = (h % 2) * HEAD_DIM
        qz_ref[h, lo:lo + HEAD_DIM, :] = qt_ref[0, h * HEAD_DIM:(h + 1) * HEAD_DIM, :]
        qz_ref[h, HEAD_DIM - lo:2 * HEAD_DIM - lo, :] = zeros


def _dsa_kernel(qt_ref, qit_ref, wit_ref, k_ref, ki2_ref, vt_ref, bias_ref, o_ref,
                key_ref, qz_ref, ot_ref, jstar_ref, *, first_block, topk):
    j = pl.program_id(1) + first_block
    n_tiles = j + 1
    neg_key = _sortable_key_np(NEG_INF)

    _fill_padded_queries(qt_ref, qz_ref)

    row_chunk = lax.broadcasted_iota(jnp.int32, (TILE, TILE), 0) // CHUNK
    col_chunk = lax.broadcasted_iota(jnp.int32, (TILE, TILE), 1) // CHUNK
    diag_adm = row_chunk <= col_chunk

    def rows_of(kt):
        return pl.ds(pl.multiple_of(kt * TILE, TILE), TILE)

    def score_tile(kt, diagonal):
        acc = jnp.zeros((TILE, TILE), _F32)
        for h in range(N_HEADS):
            kslab = ki2_ref[rows_of(kt), (h % 2) * LANES:(h % 2 + 1) * LANES]
            rel = jnp.dot(kslab, qit_ref[0, (h // 2) * LANES:(h // 2 + 1) * LANES, :],
                          preferred_element_type=_F32)
            acc = acc + wit_ref[0, h:h + 1, :] * jnp.maximum(rel, 0.0)
        acc = jnp.where(acc == 0.0, 0.0, acc)
        if diagonal:
            acc = jnp.where(diag_adm, acc, NEG_INF)
        bits = pltpu.bitcast(acc, jnp.int32)
        key_ref[rows_of(kt), :] = bits ^ ((bits >> 31) & 0x7FFFFFFF)

    def score_body(kt, carry):
        score_tile(kt, False)
        return carry

    lax.fori_loop(0, j, score_body, 0)
    score_tile(j, True)

    def count(pred):
        def body(kt, cnt):
            c = pred(key_ref[rows_of(kt), :], kt).astype(jnp.int32)
            return cnt + jnp.sum(c.reshape(TILE // 8, 8, TILE), axis=0)
        cnt8 = lax.fori_loop(0, n_tiles, body, jnp.zeros((8, TILE), jnp.int32))
        return jnp.sum(cnt8, axis=0, keepdims=True)

    def bit_body(i, t_u):
        cand_u = t_u | lax.shift_left(jnp.int32(1), 31 - i)
        cand_s = cand_u ^ _SIGN
        cnt = count(lambda keys, kt: keys >= cand_s)
        return jnp.where(cnt >= topk, cand_u, t_u)

    t_u = lax.fori_loop(0, 32, bit_body, jnp.zeros((1, TILE), jnp.int32))
    thr = t_u ^ _SIGN
    n_gt = count(lambda keys, kt: keys > thr)
    n_eq = count(lambda keys, kt: keys == thr)
    need = topk - n_gt

    big = jnp.int32(2 ** 30)
    jstar_ref[...] = jnp.full((1, TILE), big, jnp.int32)
    excess = jnp.max(jnp.where(n_eq > need, 1, 0)) > 0

    row_iota = lax.broadcasted_iota(jnp.int32, (TILE, TILE), 0)

    @pl.when(excess)
    def _():
        def idx_body(i, j0):
            cand = j0 | lax.shift_left(jnp.int32(1), 14 - i)
            cnt = count(lambda keys, kt: (keys == thr) & (row_iota + kt * TILE < cand))
            return jnp.where(cnt < need, cand, j0)
        j0 = lax.fori_loop(0, 15, idx_body, jnp.zeros((1, TILE), jnp.int32))
        jstar_ref[...] = jnp.where(n_eq > need, j0 + 1, big)

    jstar = jstar_ref[...]

    def mask_tile(kt, diagonal):
        keys = key_ref[rows_of(kt), :]
        sel = (keys > thr) | ((keys == thr) & (row_iota + kt * TILE < jstar))
        if diagonal:
            sel = sel & diag_adm
        key_ref[rows_of(kt), :] = pltpu.bitcast(jnp.where(sel, 0.0, NEG_INF).astype(_F32), jnp.int32)

    def mask_body(kt, carry):
        mask_tile(kt, False)
        return carry

    lax.fori_loop(0, j, mask_body, 0)
    mask_tile(j, True)

    for h in range(N_HEADS):
        def att_body(kt, carry, h=h):
            m, l, acc = carry
            add = bias_ref[jnp.minimum(j - kt, 2), h] + pltpu.bitcast(key_ref[rows_of(kt), :], _F32)
            return _attend_tile(k_ref[rows_of(kt), (h // 2) * LANES:(h // 2 + 1) * LANES], qz_ref[h], add,
                                vt_ref[kt, h * HEAD_DIM:(h + 1) * HEAD_DIM, :], m, l, acc)

        init = (jnp.full((1, TILE), NEG_INF, _F32), jnp.zeros((1, TILE), _F32), jnp.zeros((HEAD_DIM, TILE), _F32))
        m, l, acc = lax.fori_loop(0, n_tiles, att_body, init)
        ot_ref[h * HEAD_DIM:(h + 1) * HEAD_DIM, :] = acc / l

    o_ref[...] = ot_ref[...].T.astype(_BF16)


def _dsa(qt, qit, wit, k, ki2, vt, bias, *, batch, q_tiles, first_block, seq_tiles, topk):
    seq = seq_tiles * TILE
    once = pl.Buffered(1)
    qspec = lambda h: pl.BlockSpec((1, h, TILE), lambda b, jj: (b * q_tiles + jj, 0, 0))
    kern = functools.partial(_dsa_kernel, first_block=first_block, topk=topk)
    return pl.pallas_call(
        kern, out_shape=jax.ShapeDtypeStruct((batch * q_tiles * TILE, D_HEADS), _BF16),
        grid=(batch, q_tiles),
        in_specs=[qspec(D_HEADS), qspec(D_HEADS), qspec(N_HEADS),
                  pl.BlockSpec((seq, D_HEADS), lambda b, jj: (b, 0), pipeline_mode=once),
                  pl.BlockSpec((seq, 2 * LANES), lambda b, jj: (b, 0), pipeline_mode=once),
                  pl.BlockSpec((seq_tiles, D_HEADS, TILE), lambda b, jj: (b, 0, 0), pipeline_mode=once),
                  pl.BlockSpec(bias.shape, lambda b, jj: (0, 0, 0, 0), pipeline_mode=once)],
        out_specs=pl.BlockSpec((TILE, D_HEADS), lambda b, jj: (b * q_tiles + jj, 0)),
        scratch_shapes=[pltpu.VMEM((seq, TILE), jnp.int32),
                        pltpu.VMEM((N_HEADS, LANES, TILE), _BF16),
                        pltpu.VMEM((D_HEADS, TILE), _F32),
                        pltpu.VMEM((1, TILE), jnp.int32)],
        name="dsa", compiler_params=_params(("parallel", "arbitrary")),
    )(qt, qit, wit, k, ki2, vt, bias)


def _band_kernel(qt_ref, k0_ref, k1_ref, k2_ref, v0_ref, v1_ref, v2_ref, bias_ref, o_ref,
                 qz_ref, ot_ref, *, first_block):
    j = pl.program_id(1) + first_block
    _fill_padded_queries(qt_ref, qz_ref)
    k_refs = (k0_ref, k1_ref, k2_ref)
    v_refs = (v0_ref, v1_ref, v2_ref)
    for h in range(N_HEADS):
        m = jnp.full((1, TILE), NEG_INF, _F32)
        l = jnp.zeros((1, TILE), _F32)
        acc = jnp.zeros((HEAD_DIM, TILE), _F32)
        for t in range(3):
            off = jnp.where(j - 2 + t >= 0, 0.0, NEG_INF).astype(_F32)
            add = bias_ref[t, h] + off
            m, l, acc = _attend_tile(k_refs[t][:, (h // 2) * LANES:(h // 2 + 1) * LANES], qz_ref[h], add,
                                     v_refs[t][0, h * HEAD_DIM:(h + 1) * HEAD_DIM, :], m, l, acc)
        ot_ref[h * HEAD_DIM:(h + 1) * HEAD_DIM, :] = acc / l
    o_ref[...] = ot_ref[...].T.astype(_BF16)


def _band(qt, k, vt, bias, *, batch, q_tiles, first_block, seq_tiles):
    def kspec(t):
        return pl.BlockSpec((TILE, D_HEADS),
                            lambda b, jj: (b * seq_tiles + jnp.maximum(jj + first_block - 2 + t, 0), 0))

    def vspec(t):
        return pl.BlockSpec((1, D_HEADS, TILE),
                            lambda b, jj: (b * seq_tiles + jnp.maximum(jj + first_block - 2 + t, 0), 0, 0))

    kern = functools.partial(_band_kernel, first_block=first_block)
    return pl.pallas_call(
        kern, out_shape=jax.ShapeDtypeStruct((batch * q_tiles * TILE, D_HEADS), _BF16),
        grid=(batch, q_tiles),
        in_specs=[pl.BlockSpec((1, D_HEADS, TILE), lambda b, jj: (b * q_tiles + jj, 0, 0)),
                  kspec(0), kspec(1), kspec(2), vspec(0), vspec(1), vspec(2),
                  pl.BlockSpec(bias.shape, lambda b, jj: (0, 0, 0, 0))],
        out_specs=pl.BlockSpec((TILE, D_HEADS), lambda b, jj: (b * q_tiles + jj, 0)),
        scratch_shapes=[pltpu.VMEM((N_HEADS, LANES, TILE), _BF16), pltpu.VMEM((D_HEADS, TILE), _F32)],
        name="band", compiler_params=_params(("parallel", "arbitrary")),
    )(qt, k, k, k, vt, vt, vt, bias)


def _post_attn_kernel(oa_ref, ob_ref, x_ref, woa_ref, wob_ref, g_ref, b_ref, wr_ref, x1_ref, route_ref):
    h = jnp.dot(oa_ref[...], woa_ref[...], preferred_element_type=_F32)
    h = h + jnp.dot(ob_ref[...], wob_ref[...], preferred_element_type=_F32)
    x1 = _layer_norm_rows(DEEPNORM_ALPHA * x_ref[...] + h, g_ref[...], b_ref[...])
    x1_ref[...] = x1
    logits = jnp.dot(x1.astype(_BF16), wr_ref[...], preferred_element_type=_F32)
    lane = lax.broadcasted_iota(jnp.int32, logits.shape, 1)
    lane_f = lane.astype(_F32)
    far = jnp.float32(1e9)

    def first_max(mask):
        v = jnp.max(jnp.where(mask, logits, -jnp.inf), axis=-1, keepdims=True)
        i = jnp.min(jnp.where(mask & (logits == v), lane_f, far), axis=-1, keepdims=True)
        return v, i

    gmask = lane < N_GROUPS
    gmax, gsel = first_max(gmask)
    gsum = jnp.sum(jnp.where(gmask, jnp.exp(logits - gmax), 0.0), axis=-1, keepdims=True)
    g_gate = 1.0 / gsum
    lo = N_GROUPS + gsel * EXPERTS_PER_GROUP
    emask = (lane_f >= lo) & (lane_f < lo + EXPERTS_PER_GROUP)
    v1, i1 = first_max(emask)
    v2, i2 = first_max(emask & (lane_f != i1))
    e2 = jnp.exp(v2 - v1)
    den = 1.0 + e2
    gate1 = (1.0 / den) * g_gate
    gate2 = (e2 / den) * g_gate
    route = jnp.where(lane == 0, i1 - N_GROUPS,
                      jnp.where(lane == 1, i2 - N_GROUPS,
                                jnp.where(lane == 2, gate1, jnp.where(lane == 3, gate2, 0.0))))
    route_ref[...] = route


def _post_attn(oa, ob, x, woa, wob, g, b, wr):
    n = x.shape[0]
    rows = PROJ_ROWS
    row_spec = lambda w: pl.BlockSpec((rows, w), lambda i: (i, 0))
    full = lambda a: pl.BlockSpec(a.shape, lambda i: (0,) * a.ndim)
    return pl.pallas_call(
        _post_attn_kernel,
        out_shape=(jax.ShapeDtypeStruct((n, D_MODEL), _F32), jax.ShapeDtypeStruct((n, LANES), _F32)),
        grid=(n // rows,),
        in_specs=[row_spec(D_HEADS), row_spec(D_HEADS), row_spec(D_MODEL), full(woa), full(wob), full(g), full(b),
                  full(wr)],
        out_specs=(row_spec(D_MODEL), row_spec(LANES)),
        name="post_attn", compiler_params=_params(("parallel",)),
    )(oa, ob, x, woa, wob, g, b, wr)


def _expert_kernel(blk_e_ref, tok_ref, dst_ref, gate_ref, wg_ref, wu_ref, wd_ref, x_hbm, y_hbm,
                   xbuf, ybuf, gsem, ssem):
    del blk_e_ref

    def gather(r):
        return pltpu.make_async_copy(x_hbm.at[pl.ds(tok_ref[0, 0, r], 1)], xbuf.at[pl.ds(r, 1)], gsem)

    def scatter(r):
        return pltpu.make_async_copy(ybuf.at[pl.ds(r, 1)], y_hbm.at[pl.ds(dst_ref[0, 0, r], 1)], ssem)

    def start_gather(r, c):
        gather(r).start()
        return c

    def wait_gather(r, c):
        gather(r).wait()
        return c

    lax.fori_loop(0, EXPERT_ROWS, start_gather, 0)
    lax.fori_loop(0, EXPERT_ROWS, wait_gather, 0)

    xb = xbuf[...].astype(_BF16)
    gpre = jnp.dot(xb, wg_ref[0], preferred_element_type=_F32)
    up = jnp.dot(xb, wu_ref[0], preferred_element_type=_F32)
    hmid = (gpre * (1.0 / (1.0 + jnp.exp(-gpre)))) * up
    y = jnp.dot(hmid.astype(_BF16), wd_ref[0], preferred_element_type=_F32)
    ybuf[...] = y * gate_ref[...]

    def start_scatter(r, c):
        scatter(r).start()
        return c

    def wait_scatter(r, c):
        scatter(r).wait()
        return c

    lax.fori_loop(0, EXPERT_ROWS, start_scatter, 0)
    lax.fori_loop(0, EXPERT_ROWS, wait_scatter, 0)


def _expert_mlp(blk_e, buf_tok, buf_dst, buf_gate, wg, wu, wd, x1, n_out_rows):
    nb = blk_e.shape[0]
    rows = EXPERT_ROWS
    smem_ids = pl.BlockSpec((1, 1, rows), lambda i, e: (i, 0, 0), memory_space=pltpu.SMEM)
    wspec = lambda a: pl.BlockSpec((1,) + a.shape[1:], lambda i, e: (e[i], 0, 0))
    grid_spec = pltpu.PrefetchScalarGridSpec(
        num_scalar_prefetch=1, grid=(nb,),
        in_specs=[smem_ids, smem_ids, pl.BlockSpec((rows, 1), lambda i, e: (i, 0)),
                  wspec(wg), wspec(wu), wspec(wd), pl.BlockSpec(memory_space=pl.ANY)],
        out_specs=pl.BlockSpec(memory_space=pl.ANY),
        scratch_shapes=[pltpu.VMEM((rows, D_MODEL), _F32), pltpu.VMEM((rows, D_MODEL), _F32),
                        pltpu.SemaphoreType.DMA(()), pltpu.SemaphoreType.DMA(())])
    return pl.pallas_call(
        _expert_kernel, out_shape=jax.ShapeDtypeStruct((n_out_rows, D_MODEL), _F32), grid_spec=grid_spec,
        name="expert_mlp", compiler_params=_params(("arbitrary",)),
    )(blk_e, buf_tok.reshape(nb, 1, rows), buf_dst.reshape(nb, 1, rows), buf_gate.reshape(nb * rows, 1),
      wg, wu, wd, x1)


def _dispatch(eid, gate):
    n = eid.shape[0]
    m = 2 * n
    rows = EXPERT_ROWS
    eid_f = eid.reshape(m)
    onehot = (eid_f[:, None] == jnp.arange(N_EXPERTS, dtype=jnp.int32)[None, :]).astype(jnp.int32)
    csum = jnp.cumsum(onehot, axis=0)
    rank = jnp.sum(csum * onehot, axis=1) - 1
    counts = csum[-1]
    padded = (counts + rows - 1) // rows * rows
    pend = jnp.cumsum(padded)
    pstart = pend - padded
    slot = pstart[eid_f] + rank
    n_blocks = -(-m // rows) + N_EXPERTS
    n_slots = n_blocks * rows
    spare = m + jnp.arange(n_slots, dtype=jnp.int32) % rows
    buf_tok = jnp.zeros((n_slots,), jnp.int32).at[slot].set(jnp.arange(m, dtype=jnp.int32) // 2)
    buf_dst = spare.at[slot].set(jnp.arange(m, dtype=jnp.int32))
    buf_gate = jnp.zeros((n_slots,), _F32).at[slot].set(gate.reshape(m))
    blk_e = jnp.minimum(jnp.searchsorted(pend, jnp.arange(n_blocks, dtype=jnp.int32) * rows, side='right'),
                        N_EXPERTS - 1).astype(jnp.int32)
    return blk_e, buf_tok, buf_dst, buf_gate


def _combine_kernel(x1_ref, y_ref, g_ref, b_ref, o_ref):
    moe = y_ref[:, :D_MODEL] + y_ref[:, D_MODEL:]
    o_ref[...] = _layer_norm_rows(DEEPNORM_ALPHA * x1_ref[...] + moe, g_ref[...], b_ref[...])


def _combine(x1, y2, g, b):
    n = x1.shape[0]
    rows = PROJ_ROWS
    full = lambda a: pl.BlockSpec(a.shape, lambda i: (0,) * a.ndim)
    return pl.pallas_call(
        _combine_kernel, out_shape=jax.ShapeDtypeStruct((n, D_MODEL), _F32), grid=(n // rows,),
        in_specs=[pl.BlockSpec((rows, D_MODEL), lambda i: (i, 0)), pl.BlockSpec((rows, 2 * D_MODEL), lambda i: (i, 0)),
                  full(g), full(b)],
        out_specs=pl.BlockSpec((rows, D_MODEL), lambda i: (i, 0)),
        name="combine", compiler_params=_params(("parallel",)),
    )(x1, y2, g, b)


def _t5_bucket(rel):
    half = T5_BUCKETS // 2
    max_exact = half // 2
    ret = jnp.where(rel > 0, half, 0)
    n = jnp.abs(rel)
    nf = jnp.maximum(n, 1).astype(jnp.float32)
    large = max_exact + (jnp.log(nf / max_exact) / math.log(T5_MAX_DIST / max_exact)
                         * (half - max_exact)).astype(jnp.int32)
    large = jnp.minimum(large, half - 1)
    return ret + jnp.where(n < max_exact, n, large)


def _dsa_bias_tiles(t5_table):
    d = jnp.arange(3, dtype=jnp.int32)[:, None, None]
    r = jnp.arange(TILE, dtype=jnp.int32)[None, :, None]
    c = jnp.arange(TILE, dtype=jnp.int32)[None, None, :]
    rel = r - c - TILE * d
    return jnp.transpose(t5_table[_t5_bucket(rel)], (0, 3, 1, 2)).astype(_F32)


def _band_bias_tiles(rel_b):
    t = jnp.arange(3, dtype=jnp.int32)[:, None, None]
    r = jnp.arange(TILE, dtype=jnp.int32)[None, :, None]
    c = jnp.arange(TILE, dtype=jnp.int32)[None, None, :]
    dist = TILE * (2 - t) + c - r
    idx = jnp.clip(dist, -REL_CLIP, REL_CLIP) + REL_CLIP
    kc = CHUNKS_PER_TILE * (t - 2) + r // CHUNK
    qc = c // CHUNK
    vis = (kc <= qc) & (kc >= qc - BAND_CHUNKS)
    bias = jnp.transpose(rel_b[idx], (0, 3, 1, 2)).astype(_F32)
    return jnp.where(vis[:, None], bias, NEG_INF)


def _moe_and_norm(x1, route, wg, wu, wd, g2, b2):
    n = x1.shape[0]
    eid = route[:, 0:2].astype(jnp.int32)
    gate = route[:, 2:4]
    blk_e, buf_tok, buf_dst, buf_gate = _dispatch(eid, gate)
    y2 = _expert_mlp(blk_e, buf_tok, buf_dst, buf_gate, wg, wu, wd, x1, 2 * n + EXPERT_ROWS)
    return _combine(x1, y2.reshape(n + EXPERT_ROWS // 2, 2 * D_MODEL), g2, b2)


def kernel(x_prompt, x_sample, cache_a_k, cache_a_v, cache_a_kidx, cache_b_k, cache_b_v, t5_table, w_in, ln_idx_g,
           ln_idx_b, rel_b, w_o, ln1_g, ln1_b, w_group, w_expert, w_gate, w_up, w_down, ln2_g, ln2_b):
    assert w_in.shape[0] == DEPTH
    B, T, _ = x_prompt.shape
    S, TS, _ = x_sample.shape
    P = cache_a_k.shape[2]
    W = cache_b_k.shape[2]
    assert T % PROJ_ROWS == 0 and (S * TS) % PROJ_ROWS == 0 and P % TILE == 0 and TS <= CHUNK and W == BAND_ROWS

    w = w_in[0]
    cuts = np.cumsum([0, D_HEADS, D_HEADS, D_HEADS, N_HEADS * D_IDX, D_IDX, N_HEADS, D_HEADS, D_HEADS, D_HEADS])
    w_qa, w_ka, w_va, w_qi, w_ki, w_wi, w_qb, w_kb, w_vb = [w[:, cuts[i]:cuts[i + 1]] for i in range(9)]
    wnat = jnp.concatenate([w_ka, w_va, w_kb, w_vb], axis=1).astype(_BF16)
    wsm = jnp.pad(w_ki, ((0, 0), (0, LANES - D_IDX))).astype(_BF16)
    wt = jnp.concatenate([w_qa, w_va, w_qi, w_qb, w_vb], axis=1).T.astype(_BF16)
    wwi = jnp.pad(w_wi.T, ((0, 16 - N_HEADS), (0, 0))).astype(_BF16)
    g_idx = jnp.pad(ln_idx_g[0], (0, LANES - D_IDX)).reshape(1, LANES)
    b_idx = jnp.pad(ln_idx_b[0], (0, LANES - D_IDX)).reshape(1, LANES)
    woa = w_o[0, :D_HEADS].astype(_BF16)
    wob = w_o[0, D_HEADS:].astype(_BF16)
    wr = jnp.pad(jnp.concatenate([w_group[0], w_expert[0]], axis=1),
                 ((0, 0), (0, LANES - N_GROUPS - N_EXPERTS))).astype(_BF16)
    wg, wu, wd = w_gate[0].astype(_BF16), w_up[0].astype(_BF16), w_down[0].astype(_BF16)
    g1, b1 = ln1_g[0].reshape(1, D_MODEL), ln1_b[0].reshape(1, D_MODEL)
    g2, b2 = ln2_g[0].reshape(1, D_MODEL), ln2_b[0].reshape(1, D_MODEL)
    bias_a = _dsa_bias_tiles(t5_table)
    bias_b = _band_bias_tiles(rel_b[0])

    def finish(x_flat, oa, ob):
        x1, route = _post_attn(oa, ob, x_flat, woa, wob, g1, b1, wr)
        return _moe_and_norm(x1, route, wg, wu, wd, g2, b2)

    xp = x_prompt.reshape(B * T, D_MODEL)
    (ka, va, kb, vb, ki, kab, kbb, ki2, qat, vat, qit, qbt, vbt, wit) = _project(xp, wnat, wsm, wt, wwi, g_idx, b_idx)
    tiles = T // TILE
    oa = _dsa(qat, qit, wit, kab, ki2, vat, bias_a, batch=B, q_tiles=tiles, first_block=0, seq_tiles=tiles,
              topk=min(TOPK_MAX, T // 4))
    ob = _band(qbt, kbb, vbt, bias_b, batch=B, q_tiles=tiles, first_block=0, seq_tiles=tiles)
    y_prompt = finish(xp, oa, ob).reshape(B, T, D_MODEL)
    keep = min(BAND_ROWS, T)
    heads = lambda a, b_, t_: a.reshape(1, b_, t_, N_HEADS, HEAD_DIM)
    st_p = (heads(ka, B, T), heads(va, B, T), ki.reshape(1, B, T, D_IDX),
            heads(kb, B, T)[:, :, T - keep:], heads(vb, B, T)[:, :, T - keep:])

    xs = x_sample.reshape(S * TS, D_MODEL)
    (ka, va, kb, vb, ki, kab, kbb, ki2, qat, vat, qit, qbt, vbt, wit) = _project(xs, wnat, wsm, wt, wwi, g_idx, b_idx)
    seq_tiles = P // TILE + 1
    pad_rows = seq_tiles * TILE - P - TS

    def seq_nat(cache, new, width):
        parts = [cache.reshape(S, -1, width).astype(_BF16), new.reshape(S, TS, width).astype(_BF16),
                 jnp.zeros((S, pad_rows, width), _BF16)]
        return jnp.concatenate(parts, axis=1)

    def seq_tr(nat):
        return jnp.transpose(nat.reshape(S, seq_tiles, TILE, -1), (0, 1, 3, 2)).reshape(S * seq_tiles, -1, TILE)

    def q_tiles_of(tr, rows):
        flat = jnp.transpose(tr, (1, 0, 2)).reshape(rows, S, TS)
        return jnp.transpose(jnp.pad(flat, ((0, 0), (0, 0), (0, TILE - TS))), (1, 0, 2))

    cki = cache_a_kidx[0]
    zpad = jnp.zeros(cki.shape[:-1] + (LANES - D_IDX,), cki.dtype)
    cache_ki2 = jnp.concatenate([cki, zpad, zpad, cki], axis=-1)
    k_seq = seq_nat(cache_a_k[0], kab, D_HEADS)
    v_seq = seq_nat(cache_a_v[0], va, D_HEADS)
    ki2_seq = seq_nat(cache_ki2, ki2, 2 * LANES)
    L = P + TS
    oa = _dsa(q_tiles_of(qat, D_HEADS), q_tiles_of(qit, D_HEADS), q_tiles_of(wit, N_HEADS),
              k_seq.reshape(-1, D_HEADS), ki2_seq.reshape(-1, 2 * LANES), seq_tr(v_seq), bias_a,
              batch=S, q_tiles=1, first_block=P // TILE, seq_tiles=seq_tiles, topk=min(TOPK_MAX, L // 4))
    zeros_front = jnp.zeros((S, P - W, D_HEADS), _BF16)
    kb_seq = jnp.concatenate([zeros_front, seq_nat(cache_b_k[0], kbb, D_HEADS)], axis=1)
    vb_seq = jnp.concatenate([zeros_front, seq_nat(cache_b_v[0], vb, D_HEADS)], axis=1)
    ob = _band(q_tiles_of(qbt, D_HEADS), kb_seq.reshape(-1, D_HEADS), seq_tr(vb_seq), bias_b,
               batch=S, q_tiles=1, first_block=P // TILE, seq_tiles=seq_tiles)
    take = lambda o: o.reshape(S, TILE, D_HEADS)[:, :TS].reshape(S * TS, D_HEADS)
    y_sample = finish(xs, take(oa), take(ob)).reshape(S, TS, D_MODEL)
    band_k = jnp.concatenate([cache_b_k[0], heads(kb, S, TS)[0]], axis=1)[:, TS:]
    band_v = jnp.concatenate([cache_b_v[0], heads(vb, S, TS)[0]], axis=1)[:, TS:]
    st_s = (heads(ka, S, TS), heads(va, S, TS), ki.reshape(1, S, TS, D_IDX), band_k[None], band_v[None])

    return (y_prompt, y_sample) + st_p + st_s
```

```python
import functools
import math

import numpy as np
import jax
import jax.numpy as jnp
from jax import lax
from jax.experimental import pallas as pl
from jax.experimental.pallas import tpu as pltpu

D_MODEL = 1024
CHUNK = 64
HEAD_DIM = 64
N_HEADS = 8
D_HEADS = N_HEADS * HEAD_DIM
D_IDX = 64
TOPK_MAX = 256
T5_BUCKETS = 32
T5_MAX_DIST = 128
BAND_CHUNKS = 8
BAND_ROWS = BAND_CHUNKS * CHUNK
REL_CLIP = 128
N_GROUPS = 4
EXPERTS_PER_GROUP = 8
N_EXPERTS = N_GROUPS * EXPERTS_PER_GROUP
D_EXPERT = 512
DEPTH = 1
DEEPNORM_ALPHA = (2 * DEPTH) ** 0.25
LN_EPS = 1e-5
NEG_INF = -1e30
IDX_SCALE = (N_HEADS * D_IDX) ** -0.5
LOG2E = math.log2(math.e)
Q_SCALE = HEAD_DIM ** -0.5 * LOG2E

TILE = 256
CHUNKS_PER_TILE = TILE // CHUNK
LANES = 128
PROJ_ROWS = 512
EXPERT_ROWS = 256
V7X_VMEM_LIMIT = 56 * 1024 * 1024

_BF16 = jnp.bfloat16
_F32 = jnp.float32


def _sortable_key_np(v):
    b = np.array([v], np.float32).view(np.int32)[0]
    return np.int32(b ^ ((b >> 31) & 0x7FFFFFFF))


_SIGN = np.int32(-2 ** 31)


def _params(sem, vmem=V7X_VMEM_LIMIT):
    return pltpu.CompilerParams(dimension_semantics=sem, vmem_limit_bytes=vmem)


def _layer_norm_rows(x, g, b):
    mu = jnp.mean(x, axis=-1, keepdims=True)
    xc = x - mu
    var = jnp.mean(xc * xc, axis=-1, keepdims=True)
    return xc * lax.rsqrt(var + LN_EPS) * g + b


def _project_kernel(x_ref, wnat_ref, wsm_ref, wt_ref, wwi_ref, g_ref, b_ref,
                    ka_ref, va_ref, kb_ref, vb_ref, ki_ref,
                    kab_ref, kbb_ref, ki2_ref,
                    qat_ref, vat_ref, qit_ref, qbt_ref, vbt_ref, wit_ref):
    x = x_ref[...]
    xb = x.astype(_BF16)
    xtb = x.T.astype(_BF16)

    def nat(i):
        return jnp.dot(xb, wnat_ref[:, i * D_HEADS:(i + 1) * D_HEADS], preferred_element_type=_F32)

    ka = nat(0)
    ka_ref[...] = ka
    kab_ref[...] = ka.astype(_BF16)
    va_ref[...] = nat(1)
    kb = nat(2)
    kb_ref[...] = kb
    kbb_ref[...] = kb.astype(_BF16)
    vb_ref[...] = nat(3)

    zs = jnp.dot(xb, wsm_ref[...], preferred_element_type=_F32)
    lane = lax.broadcasted_iota(jnp.int32, zs.shape, 1)
    mu = jnp.sum(zs, axis=-1, keepdims=True) * (1.0 / D_IDX)
    zc = jnp.where(lane < D_IDX, zs - mu, 0.0)
    var = jnp.sum(zc * zc, axis=-1, keepdims=True) * (1.0 / D_IDX)
    ln = zc * lax.rsqrt(var + LN_EPS) * g_ref[...] + b_ref[...]
    ki_ref[...] = ln[:, :D_IDX]
    ki2_ref[:, :LANES] = ln.astype(_BF16)
    ki2_ref[:, LANES:] = pltpu.roll(ln, D_IDX, axis=1).astype(_BF16)

    def tr(i, scale):
        z = jnp.dot(wt_ref[i * D_HEADS:(i + 1) * D_HEADS, :], xtb, preferred_element_type=_F32)
        if scale != 1.0:
            z = z * scale
        return z.astype(_BF16)

    def put(ref, z):
        for t in range(ref.shape[0]):
            ref[t] = z[:, t * TILE:(t + 1) * TILE]

    put(qat_ref, tr(0, Q_SCALE))
    put(vat_ref, tr(1, 1.0))
    put(qit_ref, tr(2, 1.0))
    put(qbt_ref, tr(3, Q_SCALE))
    put(vbt_ref, tr(4, 1.0))
    wi = jnp.dot(wwi_ref[...], xtb, preferred_element_type=_F32) * IDX_SCALE
    put(wit_ref, wi[:N_HEADS])


def _project(x, wnat, wsm, wt, wwi, g, b):
    n = x.shape[0]
    rows = PROJ_ROWS
    tiles = rows // TILE
    nt = n // TILE
    grid = (n // rows,)
    row_spec = lambda w: pl.BlockSpec((rows, w), lambda i: (i, 0))
    full = lambda a: pl.BlockSpec(a.shape, lambda i: (0,) * a.ndim)
    tspec = lambda h: pl.BlockSpec((tiles, h, TILE), lambda i: (i, 0, 0))
    nat_f32 = jax.ShapeDtypeStruct((n, D_HEADS), _F32)
    nat_b16 = jax.ShapeDtypeStruct((n, D_HEADS), _BF16)
    tr_b16 = jax.ShapeDtypeStruct((nt, D_HEADS, TILE), _BF16)
    out_shape = (nat_f32, nat_f32, nat_f32, nat_f32, jax.ShapeDtypeStruct((n, D_IDX), _F32),
                 nat_b16, nat_b16, jax.ShapeDtypeStruct((n, 2 * LANES), _BF16),
                 tr_b16, tr_b16, tr_b16, tr_b16, tr_b16,
                 jax.ShapeDtypeStruct((nt, N_HEADS, TILE), _F32))
    out_specs = (row_spec(D_HEADS),) * 4 + (row_spec(D_IDX),) + (row_spec(D_HEADS),) * 2 + (row_spec(2 * LANES),) \
        + (tspec(D_HEADS),) * 5 + (tspec(N_HEADS),)
    return pl.pallas_call(
        _project_kernel, out_shape=out_shape, grid=grid,
        in_specs=[row_spec(D_MODEL), full(wnat), full(wsm), full(wt), full(wwi), full(g), full(b)],
        out_specs=out_specs, name="project",
        compiler_params=_params(("parallel",)),
    )(x, wnat, wsm, wt, wwi, g, b)


def _attend_tile(k_of, qz_ref, add_of, vt_of, s_ref, p_ref, al_ref, m_ref, acc_ref):
    for h in range(N_HEADS):
        s_ref[h] = jnp.dot(k_of(h), qz_ref[h], preferred_element_type=_F32) + add_of(h)
    for h in range(N_HEADS):
        s = s_ref[h]
        m = m_ref[h]
        m_new = jnp.maximum(m, jnp.max(s, axis=0, keepdims=True))
        al_ref[h] = jnp.exp2(m - m_new)
        p_ref[h] = jnp.exp2(s - m_new).astype(_BF16)
        m_ref[h] = m_new
    ones = jnp.ones((16, TILE), _BF16)
    for h in range(N_HEADS):
        lhs = jnp.concatenate([vt_of(h), ones], axis=0)
        acc_ref[h] = al_ref[h] * acc_ref[h] + jnp.dot(lhs, p_ref[h], preferred_element_type=_F32)


def _attend_init(m_ref, acc_ref):
    m_ref[...] = jnp.full(m_ref.shape, NEG_INF, _F32)
    acc_ref[...] = jnp.zeros(acc_ref.shape, _F32)


def _attend_finish(acc_ref, ot_ref, o_ref):
    for h in range(N_HEADS):
        a = acc_ref[h]
        ot_ref[h * HEAD_DIM:(h + 1) * HEAD_DIM, :] = a[:HEAD_DIM] / a[HEAD_DIM:HEAD_DIM + 1]
    o_ref[...] = ot_ref[...].T.astype(_BF16)


def _attend_scratch():
    return [pltpu.VMEM((N_HEADS, TILE, TILE), _F32), pltpu.VMEM((N_HEADS, TILE, TILE), _BF16),
            pltpu.VMEM((N_HEADS, 1, TILE), _F32), pltpu.VMEM((N_HEADS, 1, TILE), _F32),
            pltpu.VMEM((N_HEADS, HEAD_DIM + 16, TILE), _F32)]


def _fill_padded_queries(qt_ref, qz_ref):
    zeros = jnp.zeros((HEAD_DIM, TILE), _BF16)
    for h in range(N_HEADS):
        lo = (h % 2) * HEAD_DIM
        qz_ref[h, lo:lo + HEAD_DIM, :] = qt_ref[0, h * HEAD_DIM:(h + 1) * HEAD_DIM, :]
        qz_ref[h, HEAD_DIM - lo:2 * HEAD_DIM - lo, :] = zeros


def _dsa_kernel(qt_ref, qit_ref, wit_ref, k_ref, ki2_ref, vt_ref, bias_ref, o_ref,
                key_ref, qz_ref, ot_ref, jstar_ref, s_ref, p_ref, al_ref, m_ref, acc_ref, *, first_block, topk):
    j = pl.program_id(1) + first_block
    n_tiles = j + 1

    _fill_padded_queries(qt_ref, qz_ref)

    row_chunk = lax.broadcasted_iota(jnp.int32, (TILE, TILE), 0) // CHUNK
    col_chunk = lax.broadcasted_iota(jnp.int32, (TILE, TILE), 1) // CHUNK
    diag_adm = row_chunk <= col_chunk

    def rows_of(kt):
        return pl.ds(pl.multiple_of(kt * TILE, TILE), TILE)

    def score_tile(kt, diagonal):
        acc = jnp.zeros((TILE, TILE), _F32)
        for h in range(N_HEADS):
            kslab = ki2_ref[rows_of(kt), (h % 2) * LANES:(h % 2 + 1) * LANES]
            rel = jnp.dot(kslab, qit_ref[0, (h // 2) * LANES:(h // 2 + 1) * LANES, :],
                          preferred_element_type=_F32)
            acc = acc + wit_ref[0, h:h + 1, :] * jnp.maximum(rel, 0.0)
        acc = jnp.where(acc == 0.0, 0.0, acc)
        if diagonal:
            acc = jnp.where(diag_adm, acc, NEG_INF)
        bits = pltpu.bitcast(acc, jnp.int32)
        key_ref[rows_of(kt), :] = bits ^ ((bits >> 31) & 0x7FFFFFFF)

    def score_body(kt, carry):
        score_tile(kt, False)
        return carry

    lax.fori_loop(0, j, score_body, 0)
    score_tile(j, True)

    def count(pred):
        def body(kt, cnt):
            c = pred(key_ref[rows_of(kt), :], kt).astype(jnp.int32)
            return cnt + jnp.sum(c.reshape(TILE // 8, 8, TILE), axis=0)
        cnt8 = lax.fori_loop(0, n_tiles, body, jnp.zeros((8, TILE), jnp.int32))
        return jnp.sum(cnt8, axis=0, keepdims=True)

    def bit_body(i, t_u):
        cand_u = t_u | lax.shift_left(jnp.int32(1), 31 - i)
        cand_s = cand_u ^ _SIGN
        cnt = count(lambda keys, kt: keys >= cand_s)
        return jnp.where(cnt >= topk, cand_u, t_u)

    t_u = lax.fori_loop(0, 32, bit_body, jnp.zeros((1, TILE), jnp.int32))
    thr = t_u ^ _SIGN
    n_gt = count(lambda keys, kt: keys > thr)
    n_eq = count(lambda keys, kt: keys == thr)
    need = topk - n_gt

    big = jnp.int32(2 ** 30)
    jstar_ref[...] = jnp.full((1, TILE), big, jnp.int32)
    excess = jnp.max(jnp.where(n_eq > need, 1, 0)) > 0

    row_iota = lax.broadcasted_iota(jnp.int32, (TILE, TILE), 0)

    @pl.when(excess)
    def _():
        def idx_body(i, j0):
            cand = j0 | lax.shift_left(jnp.int32(1), 14 - i)
            cnt = count(lambda keys, kt: (keys == thr) & (row_iota + kt * TILE < cand))
            return jnp.where(cnt < need, cand, j0)
        j0 = lax.fori_loop(0, 15, idx_body, jnp.zeros((1, TILE), jnp.int32))
        jstar_ref[...] = jnp.where(n_eq > need, j0 + 1, big)

    jstar = jstar_ref[...]

    def mask_tile(kt, diagonal):
        keys = key_ref[rows_of(kt), :]
        sel = (keys > thr) | ((keys == thr) & (row_iota + kt * TILE < jstar))
        if diagonal:
            sel = sel & diag_adm
        key_ref[rows_of(kt), :] = pltpu.bitcast(jnp.where(sel, 0.0, NEG_INF).astype(_F32), jnp.int32)

    def mask_body(kt, carry):
        mask_tile(kt, False)
        return carry

    lax.fori_loop(0, j, mask_body, 0)
    mask_tile(j, True)

    _attend_init(m_ref, acc_ref)

    def attend(kt, near):
        mask = pltpu.bitcast(key_ref[rows_of(kt), :], _F32)
        add_of = (lambda h: mask) if near is None else (lambda h: mask + bias_ref[near, h])
        _attend_tile(lambda h: k_ref[rows_of(kt), (h // 2) * LANES:(h // 2 + 1) * LANES], qz_ref, add_of,
                     lambda h: vt_ref[kt, h * HEAD_DIM:(h + 1) * HEAD_DIM, :], s_ref, p_ref, al_ref, m_ref, acc_ref)

    def far_body(kt, carry):
        attend(kt, None)
        return carry

    lax.fori_loop(0, jnp.maximum(j - 1, 0), far_body, 0)

    @pl.when(j >= 1)
    def _():
        attend(j - 1, 1)

    attend(j, 0)
    _attend_finish(acc_ref, ot_ref, o_ref)


def _dsa(qt, qit, wit, k, ki2, vt, bias, *, batch, q_tiles, first_block, seq_tiles, topk):
    seq = seq_tiles * TILE
    once = pl.Buffered(1)
    qspec = lambda h: pl.BlockSpec((1, h, TILE), lambda b, jj: (b * q_tiles + jj, 0, 0))
    kern = functools.partial(_dsa_kernel, first_block=first_block, topk=topk)
    return pl.pallas_call(
        kern, out_shape=jax.ShapeDtypeStruct((batch * q_tiles * TILE, D_HEADS), _BF16),
        grid=(batch, q_tiles),
        in_specs=[qspec(D_HEADS), qspec(D_HEADS), qspec(N_HEADS),
                  pl.BlockSpec((seq, D_HEADS), lambda b, jj: (b, 0), pipeline_mode=once),
                  pl.BlockSpec((seq, 2 * LANES), lambda b, jj: (b, 0), pipeline_mode=once),
                  pl.BlockSpec((seq_tiles, D_HEADS, TILE), lambda b, jj: (b, 0, 0), pipeline_mode=once),
                  pl.BlockSpec(bias.shape, lambda b, jj: (0, 0, 0, 0), pipeline_mode=once)],
        out_specs=pl.BlockSpec((TILE, D_HEADS), lambda b, jj: (b * q_tiles + jj, 0)),
        scratch_shapes=[pltpu.VMEM((seq, TILE), jnp.int32),
                        pltpu.VMEM((N_HEADS, LANES, TILE), _BF16),
                        pltpu.VMEM((D_HEADS, TILE), _F32),
                        pltpu.VMEM((1, TILE), jnp.int32)] + _attend_scratch(),
        name="dsa", compiler_params=_params(("parallel", "arbitrary")),
    )(qt, qit, wit, k, ki2, vt, bias)


def _band_kernel(qt_ref, k0_ref, k1_ref, k2_ref, v0_ref, v1_ref, v2_ref, bias_ref, o_ref,
                 qz_ref, ot_ref, s_ref, p_ref, al_ref, m_ref, acc_ref, *, first_block):
    j = pl.program_id(1) + first_block
    _fill_padded_queries(qt_ref, qz_ref)
    _attend_init(m_ref, acc_ref)
    k_refs = (k0_ref, k1_ref, k2_ref)
    v_refs = (v0_ref, v1_ref, v2_ref)
    for t in range(3):
        off = jnp.where(j - 2 + t >= 0, 0.0, NEG_INF).astype(_F32)
        _attend_tile(lambda h: k_refs[t][:, (h // 2) * LANES:(h // 2 + 1) * LANES], qz_ref,
                     lambda h: bias_ref[t, h] + off,
                     lambda h: v_refs[t][0, h * HEAD_DIM:(h + 1) * HEAD_DIM, :], s_ref, p_ref, al_ref, m_ref, acc_ref)
    _attend_finish(acc_ref, ot_ref, o_ref)


def _band(qt, k, vt, bias, *, batch, q_tiles, first_block, seq_tiles):
    def kspec(t):
        return pl.BlockSpec((TILE, D_HEADS),
                            lambda b, jj: (b * seq_tiles + jnp.maximum(jj + first_block - 2 + t, 0), 0))

    def vspec(t):
        return pl.BlockSpec((1, D_HEADS, TILE),
                            lambda b, jj: (b * seq_tiles + jnp.maximum(jj + first_block - 2 + t, 0), 0, 0))

    kern = functools.partial(_band_kernel, first_block=first_block)
    return pl.pallas_call(
        kern, out_shape=jax.ShapeDtypeStruct((batch * q_tiles * TILE, D_HEADS), _BF16),
        grid=(batch, q_tiles),
        in_specs=[pl.BlockSpec((1, D_HEADS, TILE), lambda b, jj: (b * q_tiles + jj, 0, 0)),
                  kspec(0), kspec(1), kspec(2), vspec(0), vspec(1), vspec(2),
                  pl.BlockSpec(bias.shape, lambda b, jj: (0, 0, 0, 0))],
        out_specs=pl.BlockSpec((TILE, D_HEADS), lambda b, jj: (b * q_tiles + jj, 0)),
        scratch_shapes=[pltpu.VMEM((N_HEADS, LANES, TILE), _BF16), pltpu.VMEM((D_HEADS, TILE), _F32)]
        + _attend_scratch(),
        name="band", compiler_params=_params(("parallel", "arbitrary")),
    )(qt, k, k, k, vt, vt, vt, bias)


def _post_attn_kernel(oa_ref, ob_ref, x_ref, woa_ref, wob_ref, g_ref, b_ref, wr_ref, x1_ref, route_ref):
    h = jnp.dot(oa_ref[...], woa_ref[...], preferred_element_type=_F32)
    h = h + jnp.dot(ob_ref[...], wob_ref[...], preferred_element_type=_F32)
    x1 = _layer_norm_rows(DEEPNORM_ALPHA * x_ref[...] + h, g_ref[...], b_ref[...])
    x1_ref[...] = x1
    logits = jnp.dot(x1.astype(_BF16), wr_ref[...], preferred_element_type=_F32)
    lane = lax.broadcasted_iota(jnp.int32, logits.shape, 1)
    lane_f = lane.astype(_F32)
    far = jnp.float32(1e9)

    def first_max(mask):
        v = jnp.max(jnp.where(mask, logits, -jnp.inf), axis=-1, keepdims=True)
        i = jnp.min(jnp.where(mask & (logits == v), lane_f, far), axis=-1, keepdims=True)
        return v, i

    gmask = lane < N_GROUPS
    gmax, gsel = first_max(gmask)
    gsum = jnp.sum(jnp.where(gmask, jnp.exp(logits - gmax), 0.0), axis=-1, keepdims=True)
    g_gate = 1.0 / gsum
    lo = N_GROUPS + gsel * EXPERTS_PER_GROUP
    emask = (lane_f >= lo) & (lane_f < lo + EXPERTS_PER_GROUP)
    v1, i1 = first_max(emask)
    v2, i2 = first_max(emask & (lane_f != i1))
    e2 = jnp.exp(v2 - v1)
    den = 1.0 + e2
    gate1 = (1.0 / den) * g_gate
    gate2 = (e2 / den) * g_gate
    route = jnp.where(lane == 0, i1 - N_GROUPS,
                      jnp.where(lane == 1, i2 - N_GROUPS,
                                jnp.where(lane == 2, gate1, jnp.where(lane == 3, gate2, 0.0))))
    route_ref[...] = route


def _post_attn(oa, ob, x, woa, wob, g, b, wr):
    n = x.shape[0]
    rows = PROJ_ROWS
    row_spec = lambda w: pl.BlockSpec((rows, w), lambda i: (i, 0))
    full = lambda a: pl.BlockSpec(a.shape, lambda i: (0,) * a.ndim)
    return pl.pallas_call(
        _post_attn_kernel,
        out_shape=(jax.ShapeDtypeStruct((n, D_MODEL), _F32), jax.ShapeDtypeStruct((n, LANES), _F32)),
        grid=(n // rows,),
        in_specs=[row_spec(D_HEADS), row_spec(D_HEADS), row_spec(D_MODEL), full(woa), full(wob), full(g), full(b),
                  full(wr)],
        out_specs=(row_spec(D_MODEL), row_spec(LANES)),
        name="post_attn", compiler_params=_params(("parallel",)),
    )(oa, ob, x, woa, wob, g, b, wr)


def _expert_kernel(blk_e_ref, tok_ref, dst_ref, gate_ref, wg_ref, wu_ref, wd_ref, x_hbm, y_hbm,
                   xbuf, ybuf, gsem, ssem):
    del blk_e_ref

    def gather(r):
        return pltpu.make_async_copy(x_hbm.at[pl.ds(tok_ref[0, 0, r], 1)], xbuf.at[pl.ds(r, 1)], gsem)

    def scatter(r):
        return pltpu.make_async_copy(ybuf.at[pl.ds(r, 1)], y_hbm.at[pl.ds(dst_ref[0, 0, r], 1)], ssem)

    def start_gather(r, c):
        gather(r).start()
        return c

    def wait_gather(r, c):
        gather(r).wait()
        return c

    lax.fori_loop(0, EXPERT_ROWS, start_gather, 0)
    lax.fori_loop(0, EXPERT_ROWS, wait_gather, 0)

    xb = xbuf[...].astype(_BF16)
    gpre = jnp.dot(xb, wg_ref[0], preferred_element_type=_F32)
    up = jnp.dot(xb, wu_ref[0], preferred_element_type=_F32)
    hmid = (gpre * (1.0 / (1.0 + jnp.exp(-gpre)))) * up
    y = jnp.dot(hmid.astype(_BF16), wd_ref[0], preferred_element_type=_F32)
    ybuf[...] = y * gate_ref[...]

    def start_scatter(r, c):
        scatter(r).start()
        return c

    def wait_scatter(r, c):
        scatter(r).wait()
        return c

    lax.fori_loop(0, EXPERT_ROWS, start_scatter, 0)
    lax.fori_loop(0, EXPERT_ROWS, wait_scatter, 0)


def _expert_mlp(blk_e, buf_tok, buf_dst, buf_gate, wg, wu, wd, x1, n_out_rows):
    nb = blk_e.shape[0]
    rows = EXPERT_ROWS
    smem_ids = pl.BlockSpec((1, 1, rows), lambda i, e: (i, 0, 0), memory_space=pltpu.SMEM)
    wspec = lambda a: pl.BlockSpec((1,) + a.shape[1:], lambda i, e: (e[i], 0, 0))
    grid_spec = pltpu.PrefetchScalarGridSpec(
        num_scalar_prefetch=1, grid=(nb,),
        in_specs=[smem_ids, smem_ids, pl.BlockSpec((rows, 1), lambda i, e: (i, 0)),
                  wspec(wg), wspec(wu), wspec(wd), pl.BlockSpec(memory_space=pl.ANY)],
        out_specs=pl.BlockSpec(memory_space=pl.ANY),
        scratch_shapes=[pltpu.VMEM((rows, D_MODEL), _F32), pltpu.VMEM((rows, D_MODEL), _F32),
                        pltpu.SemaphoreType.DMA(()), pltpu.SemaphoreType.DMA(())])
    return pl.pallas_call(
        _expert_kernel, out_shape=jax.ShapeDtypeStruct((n_out_rows, D_MODEL), _F32), grid_spec=grid_spec,
        name="expert_mlp", compiler_params=_params(("arbitrary",)),
    )(blk_e, buf_tok.reshape(nb, 1, rows), buf_dst.reshape(nb, 1, rows), buf_gate.reshape(nb * rows, 1),
      wg, wu, wd, x1)


def _dispatch(eid, gate):
    n = eid.shape[0]
    m = 2 * n
    rows = EXPERT_ROWS
    eid_f = eid.reshape(m)
    onehot = (eid_f[:, None] == jnp.arange(N_EXPERTS, dtype=jnp.int32)[None, :]).astype(jnp.int32)
    csum = jnp.cumsum(onehot, axis=0)
    rank = jnp.sum(csum * onehot, axis=1) - 1
    counts = csum[-1]
    padded = (counts + rows - 1) // rows * rows
    pend = jnp.cumsum(padded)
    pstart = pend - padded
    slot = pstart[eid_f] + rank
    n_blocks = -(-m // rows) + N_EXPERTS
    n_slots = n_blocks * rows
    spare = m + jnp.arange(n_slots, dtype=jnp.int32) % rows
    buf_tok = jnp.zeros((n_slots,), jnp.int32).at[slot].set(jnp.arange(m, dtype=jnp.int32) // 2)
    buf_dst = spare.at[slot].set(jnp.arange(m, dtype=jnp.int32))
    buf_gate = jnp.zeros((n_slots,), _F32).at[slot].set(gate.reshape(m))
    blk_e = jnp.minimum(jnp.searchsorted(pend, jnp.arange(n_blocks, dtype=jnp.int32) * rows, side='right'),
                        N_EXPERTS - 1).astype(jnp.int32)
    return blk_e, buf_tok, buf_dst, buf_gate


def _combine_kernel(x1_ref, y_ref, g_ref, b_ref, o_ref):
    moe = y_ref[:, :D_MODEL] + y_ref[:, D_MODEL:]
    o_ref[...] = _layer_norm_rows(DEEPNORM_ALPHA * x1_ref[...] + moe, g_ref[...], b_ref[...])


def _combine(x1, y2, g, b):
    n = x1.shape[0]
    rows = PROJ_ROWS
    full = lambda a: pl.BlockSpec(a.shape, lambda i: (0,) * a.ndim)
    return pl.pallas_call(
        _combine_kernel, out_shape=jax.ShapeDtypeStruct((n, D_MODEL), _F32), grid=(n // rows,),
        in_specs=[pl.BlockSpec((rows, D_MODEL), lambda i: (i, 0)), pl.BlockSpec((rows, 2 * D_MODEL), lambda i: (i, 0)),
                  full(g), full(b)],
        out_specs=pl.BlockSpec((rows, D_MODEL), lambda i: (i, 0)),
        name="combine", compiler_params=_params(("parallel",)),
    )(x1, y2, g, b)


def _t5_bucket(rel):
    half = T5_BUCKETS // 2
    max_exact = half // 2
    ret = jnp.where(rel > 0, half, 0)
    n = jnp.abs(rel)
    nf = jnp.maximum(n, 1).astype(jnp.float32)
    large = max_exact + (jnp.log(nf / max_exact) / math.log(T5_MAX_DIST / max_exact)
                         * (half - max_exact)).astype(jnp.int32)
    large = jnp.minimum(large, half - 1)
    return ret + jnp.where(n < max_exact, n, large)


def _dsa_bias_tiles(t5_table):
    d = jnp.arange(2, dtype=jnp.int32)[:, None, None]
    r = jnp.arange(TILE, dtype=jnp.int32)[None, :, None]
    c = jnp.arange(TILE, dtype=jnp.int32)[None, None, :]
    rel = r - c - TILE * d
    far = t5_table[_t5_bucket(jnp.int32(-2 * TILE))]
    return jnp.transpose((t5_table[_t5_bucket(rel)] - far) * LOG2E, (0, 3, 1, 2)).astype(_F32)


def _band_bias_tiles(rel_b):
    t = jnp.arange(3, dtype=jnp.int32)[:, None, None]
    r = jnp.arange(TILE, dtype=jnp.int32)[None, :, None]
    c = jnp.arange(TILE, dtype=jnp.int32)[None, None, :]
    dist = TILE * (2 - t) + c - r
    idx = jnp.clip(dist, -REL_CLIP, REL_CLIP) + REL_CLIP
    kc = CHUNKS_PER_TILE * (t - 2) + r // CHUNK
    qc = c // CHUNK
    vis = (kc <= qc) & (kc >= qc - BAND_CHUNKS)
    bias = jnp.transpose(rel_b[idx] * LOG2E, (0, 3, 1, 2)).astype(_F32)
    return jnp.where(vis[:, None], bias, NEG_INF)


def _moe_and_norm(x1, route, wg, wu, wd, g2, b2):
    n = x1.shape[0]
    eid = route[:, 0:2].astype(jnp.int32)
    gate = route[:, 2:4]
    blk_e, buf_tok, buf_dst, buf_gate = _dispatch(eid, gate)
    y2 = _expert_mlp(blk_e, buf_tok, buf_dst, buf_gate, wg, wu, wd, x1, 2 * n + EXPERT_ROWS)
    return _combine(x1, y2.reshape(n + EXPERT_ROWS // 2, 2 * D_MODEL), g2, b2)


def kernel(x_prompt, x_sample, cache_a_k, cache_a_v, cache_a_kidx, cache_b_k, cache_b_v, t5_table, w_in, ln_idx_g,
           ln_idx_b, rel_b, w_o, ln1_g, ln1_b, w_group, w_expert, w_gate, w_up, w_down, ln2_g, ln2_b):
    assert w_in.shape[0] == DEPTH
    B, T, _ = x_prompt.shape
    S, TS, _ = x_sample.shape
    P = cache_a_k.shape[2]
    W = cache_b_k.shape[2]
    assert T % PROJ_ROWS == 0 and (S * TS) % PROJ_ROWS == 0 and P % TILE == 0 and TS <= CHUNK and W == BAND_ROWS

    w = w_in[0]
    cuts = np.cumsum([0, D_HEADS, D_HEADS, D_HEADS, N_HEADS * D_IDX, D_IDX, N_HEADS, D_HEADS, D_HEADS, D_HEADS])
    w_qa, w_ka, w_va, w_qi, w_ki, w_wi, w_qb, w_kb, w_vb = [w[:, cuts[i]:cuts[i + 1]] for i in range(9)]
    wnat = jnp.concatenate([w_ka, w_va, w_kb, w_vb], axis=1).astype(_BF16)
    wsm = jnp.pad(w_ki, ((0, 0), (0, LANES - D_IDX))).astype(_BF16)
    wt = jnp.concatenate([w_qa, w_va, w_qi, w_qb, w_vb], axis=1).T.astype(_BF16)
    wwi = jnp.pad(w_wi.T, ((0, 16 - N_HEADS), (0, 0))).astype(_BF16)
    g_idx = jnp.pad(ln_idx_g[0], (0, LANES - D_IDX)).reshape(1, LANES)
    b_idx = jnp.pad(ln_idx_b[0], (0, LANES - D_IDX)).reshape(1, LANES)
    woa = w_o[0, :D_HEADS].astype(_BF16)
    wob = w_o[0, D_HEADS:].astype(_BF16)
    wr = jnp.pad(jnp.concatenate([w_group[0], w_expert[0]], axis=1),
                 ((0, 0), (0, LANES - N_GROUPS - N_EXPERTS))).astype(_BF16)
    wg, wu, wd = w_gate[0].astype(_BF16), w_up[0].astype(_BF16), w_down[0].astype(_BF16)
    g1, b1 = ln1_g[0].reshape(1, D_MODEL), ln1_b[0].reshape(1, D_MODEL)
    g2, b2 = ln2_g[0].reshape(1, D_MODEL), ln2_b[0].reshape(1, D_MODEL)
    bias_a = _dsa_bias_tiles(t5_table)
    bias_b = _band_bias_tiles(rel_b[0])

    def finish(x_flat, oa, ob):
        x1, route = _post_attn(oa, ob, x_flat, woa, wob, g1, b1, wr)
        return _moe_and_norm(x1, route, wg, wu, wd, g2, b2)

    xp = x_prompt.reshape(B * T, D_MODEL)
    (ka, va, kb, vb, ki, kab, kbb, ki2, qat, vat, qit, qbt, vbt, wit) = _project(xp, wnat, wsm, wt, wwi, g_idx, b_idx)
    tiles = T // TILE
    oa = _dsa(qat, qit, wit, kab, ki2, vat, bias_a, batch=B, q_tiles=tiles, first_block=0, seq_tiles=tiles,
              topk=min(TOPK_MAX, T // 4))
    ob = _band(qbt, kbb, vbt, bias_b, batch=B, q_tiles=tiles, first_block=0, seq_tiles=tiles)
    y_prompt = finish(xp, oa, ob).reshape(B, T, D_MODEL)
    keep = min(BAND_ROWS, T)
    heads = lambda a, b_, t_: a.reshape(1, b_, t_, N_HEADS, HEAD_DIM)
    st_p = (heads(ka, B, T), heads(va, B, T), ki.reshape(1, B, T, D_IDX),
            heads(kb, B, T)[:, :, T - keep:], heads(vb, B, T)[:, :, T - keep:])

    xs = x_sample.reshape(S * TS, D_MODEL)
    (ka, va, kb, vb, ki, kab, kbb, ki2, qat, vat, qit, qbt, vbt, wit) = _project(xs, wnat, wsm, wt, wwi, g_idx, b_idx)
    seq_tiles = P // TILE + 1
    pad_rows = seq_tiles * TILE - P - TS

    def seq_nat(cache, new, width):
        parts = [cache.reshape(S, -1, width).astype(_BF16), new.reshape(S, TS, width).astype(_BF16),
                 jnp.zeros((S, pad_rows, width), _BF16)]
        return jnp.concatenate(parts, axis=1)

    def seq_tr(nat):
        return jnp.transpose(nat.reshape(S, seq_tiles, TILE, -1), (0, 1, 3, 2)).reshape(S * seq_tiles, -1, TILE)

    def q_tiles_of(tr, rows):
        flat = jnp.transpose(tr, (1, 0, 2)).reshape(rows, S, TS)
        return jnp.transpose(jnp.pad(flat, ((0, 0), (0, 0), (0, TILE - TS))), (1, 0, 2))

    cki = cache_a_kidx[0]
    zpad = jnp.zeros(cki.shape[:-1] + (LANES - D_IDX,), cki.dtype)
    cache_ki2 = jnp.concatenate([cki, zpad, zpad, cki], axis=-1)
    k_seq = seq_nat(cache_a_k[0], kab, D_HEADS)
    v_seq = seq_nat(cache_a_v[0], va, D_HEADS)
    ki2_seq = seq_nat(cache_ki2, ki2, 2 * LANES)
    L = P + TS
    oa = _dsa(q_tiles_of(qat, D_HEADS), q_tiles_of(qit, D_HEADS), q_tiles_of(wit, N_HEADS),
              k_seq.reshape(-1, D_HEADS), ki2_seq.reshape(-1, 2 * LANES), seq_tr(v_seq), bias_a,
              batch=S, q_tiles=1, first_block=P // TILE, seq_tiles=seq_tiles, topk=min(TOPK_MAX, L // 4))
    zeros_front = jnp.zeros((S, P - W, D_HEADS), _BF16)
    kb_seq = jnp.concatenate([zeros_front, seq_nat(cache_b_k[0], kbb, D_HEADS)], axis=1)
    vb_seq = jnp.concatenate([zeros_front, seq_nat(cache_b_v[0], vb, D_HEADS)], axis=1)
    ob = _band(q_tiles_of(qbt, D_HEADS), kb_seq.reshape(-1, D_HEADS), seq_tr(vb_seq), bias_b,
               batch=S, q_tiles=1, first_block=P // TILE, seq_tiles=seq_tiles)
    take = lambda o: o.reshape(S, TILE, D_HEADS)[:, :TS].reshape(S * TS, D_HEADS)
    y_sample = finish(xs, take(oa), take(ob)).reshape(S, TS, D_MODEL)
    band_k = jnp.concatenate([cache_b_k[0], heads(kb, S, TS)[0]], axis=1)[:, TS:]
    band_v = jnp.concatenate([cache_b_v[0], heads(vb, S, TS)[0]], axis=1)[:, TS:]
    st_s = (heads(ka, S, TS), heads(va, S, TS), ki.reshape(1, S, TS, D_IDX), band_k[None], band_v[None])

    return (y_prompt, y_sample) + st_p + st_s
```

```python
import functools
import math

import numpy as np
import jax
import jax.numpy as jnp
from jax import lax
from jax.experimental import pallas as pl
from jax.experimental.pallas import tpu as pltpu

D_MODEL = 1024
CHUNK = 64
HEAD_DIM = 64
N_HEADS = 8
D_HEADS = N_HEADS * HEAD_DIM
D_IDX = 64
TOPK_MAX = 256
T5_BUCKETS = 32
T5_MAX_DIST = 128
BAND_CHUNKS = 8
BAND_ROWS = BAND_CHUNKS * CHUNK
REL_CLIP = 128
N_GROUPS = 4
EXPERTS_PER_GROUP = 8
N_EXPERTS = N_GROUPS * EXPERTS_PER_GROUP
D_EXPERT = 512
DEPTH = 1
DEEPNORM_ALPHA = (2 * DEPTH) ** 0.25
LN_EPS = 1e-5
NEG_INF = -1e30
IDX_SCALE = (N_HEADS * D_IDX) ** -0.5
LOG2E = math.log2(math.e)
Q_SCALE = HEAD_DIM ** -0.5 * LOG2E

TILE = 256
CHUNKS_PER_TILE = TILE // CHUNK
LANES = 128
PROJ_ROWS = 512
EXPERT_ROWS = 256
V7X_VMEM_LIMIT = 56 * 1024 * 1024

_BF16 = jnp.bfloat16
_MASK16 = jnp.bfloat16
_F32 = jnp.float32


HALF = 2 ** 15


def _params(sem, vmem=V7X_VMEM_LIMIT):
    return pltpu.CompilerParams(dimension_semantics=sem, vmem_limit_bytes=vmem)


def _layer_norm_rows(x, g, b):
    mu = jnp.mean(x, axis=-1, keepdims=True)
    xc = x - mu
    var = jnp.mean(xc * xc, axis=-1, keepdims=True)
    return xc * lax.rsqrt(var + LN_EPS) * g + b


def _project_kernel(x_ref, wnat_ref, wsm_ref, wt_ref, wwi_ref, g_ref, b_ref,
                    ka_ref, va_ref, kb_ref, vb_ref, ki_ref,
                    kab_ref, kbb_ref, ki2_ref,
                    qat_ref, vat_ref, qit_ref, qbt_ref, vbt_ref, wit_ref):
    x = x_ref[...]
    xb = x.astype(_BF16)
    xtb = x.T.astype(_BF16)

    def nat(i):
        return jnp.dot(xb, wnat_ref[:, i * D_HEADS:(i + 1) * D_HEADS], preferred_element_type=_F32)

    ka = nat(0)
    ka_ref[...] = ka
    kab_ref[...] = ka.astype(_BF16)
    va_ref[...] = nat(1)
    kb = nat(2)
    kb_ref[...] = kb
    kbb_ref[...] = kb.astype(_BF16)
    vb_ref[...] = nat(3)

    zs = jnp.dot(xb, wsm_ref[...], preferred_element_type=_F32)
    lane = lax.broadcasted_iota(jnp.int32, zs.shape, 1)
    mu = jnp.sum(zs, axis=-1, keepdims=True) * (1.0 / D_IDX)
    zc = jnp.where(lane < D_IDX, zs - mu, 0.0)
    var = jnp.sum(zc * zc, axis=-1, keepdims=True) * (1.0 / D_IDX)
    ln = zc * lax.rsqrt(var + LN_EPS) * g_ref[...] + b_ref[...]
    ki_ref[...] = ln[:, :D_IDX]
    ki2_ref[:, :LANES] = ln.astype(_BF16)
    ki2_ref[:, LANES:] = pltpu.roll(ln, D_IDX, axis=1).astype(_BF16)

    def tr(i, scale):
        z = jnp.dot(wt_ref[i * D_HEADS:(i + 1) * D_HEADS, :], xtb, preferred_element_type=_F32)
        if scale != 1.0:
            z = z * scale
        return z.astype(_BF16)

    def put(ref, z):
        for t in range(ref.shape[0]):
            ref[t] = z[:, t * TILE:(t + 1) * TILE]

    put(qat_ref, tr(0, Q_SCALE))
    put(vat_ref, tr(1, 1.0))
    put(qit_ref, tr(2, 1.0))
    put(qbt_ref, tr(3, Q_SCALE))
    put(vbt_ref, tr(4, 1.0))
    wi = jnp.dot(wwi_ref[...], xtb, preferred_element_type=_F32) * IDX_SCALE
    put(wit_ref, wi[:N_HEADS])


def _project(x, wnat, wsm, wt, wwi, g, b):
    n = x.shape[0]
    rows = PROJ_ROWS
    tiles = rows // TILE
    nt = n // TILE
    grid = (n // rows,)
    row_spec = lambda w: pl.BlockSpec((rows, w), lambda i: (i, 0))
    full = lambda a: pl.BlockSpec(a.shape, lambda i: (0,) * a.ndim)
    tspec = lambda h: pl.BlockSpec((tiles, h, TILE), lambda i: (i, 0, 0))
    nat_f32 = jax.ShapeDtypeStruct((n, D_HEADS), _F32)
    nat_b16 = jax.ShapeDtypeStruct((n, D_HEADS), _BF16)
    tr_b16 = jax.ShapeDtypeStruct((nt, D_HEADS, TILE), _BF16)
    out_shape = (nat_f32, nat_f32, nat_f32, nat_f32, jax.ShapeDtypeStruct((n, D_IDX), _F32),
                 nat_b16, nat_b16, jax.ShapeDtypeStruct((n, 2 * LANES), _BF16),
                 tr_b16, tr_b16, tr_b16, tr_b16, tr_b16,
                 jax.ShapeDtypeStruct((nt, N_HEADS, TILE), _F32))
    out_specs = (row_spec(D_HEADS),) * 4 + (row_spec(D_IDX),) + (row_spec(D_HEADS),) * 2 + (row_spec(2 * LANES),) \
        + (tspec(D_HEADS),) * 5 + (tspec(N_HEADS),)
    return pl.pallas_call(
        _project_kernel, out_shape=out_shape, grid=grid,
        in_specs=[row_spec(D_MODEL), full(wnat), full(wsm), full(wt), full(wwi), full(g), full(b)],
        out_specs=out_specs, name="project",
        compiler_params=_params(("parallel",)),
    )(x, wnat, wsm, wt, wwi, g, b)


def _attend_tile(k_of, qz_ref, add_of, vt_of, s_ref, p_ref, al_ref, m_ref, acc_ref):
    for h in range(N_HEADS):
        s_ref[h] = jnp.dot(k_of(h), qz_ref[h], preferred_element_type=_F32) + add_of(h)
    for h in range(N_HEADS):
        s = s_ref[h]
        m = m_ref[h]
        m_new = jnp.maximum(m, jnp.max(s, axis=0, keepdims=True))
        al_ref[h] = jnp.exp2(m - m_new)
        p_ref[h] = jnp.exp2(s - m_new).astype(_BF16)
        m_ref[h] = m_new
    ones = jnp.ones((16, TILE), _BF16)
    for h in range(N_HEADS):
        lhs = jnp.concatenate([vt_of(h), ones], axis=0)
        acc_ref[h] = al_ref[h] * acc_ref[h] + jnp.dot(lhs, p_ref[h], preferred_element_type=_F32)


def _attend_init(m_ref, acc_ref):
    m_ref[...] = jnp.full(m_ref.shape, NEG_INF, _F32)
    acc_ref[...] = jnp.zeros(acc_ref.shape, _F32)


def _attend_finish(acc_ref, ot_ref, o_ref):
    for h in range(N_HEADS):
        a = acc_ref[h]
        ot_ref[h * HEAD_DIM:(h + 1) * HEAD_DIM, :] = a[:HEAD_DIM] / a[HEAD_DIM:HEAD_DIM + 1]
    o_ref[...] = ot_ref[...].T.astype(_BF16)


def _attend_scratch():
    return [pltpu.VMEM((N_HEADS, TILE, TILE), _F32), pltpu.VMEM((N_HEADS, TILE, TILE), _BF16),
            pltpu.VMEM((N_HEADS, 1, TILE), _F32), pltpu.VMEM((N_HEADS, 1, TILE), _F32),
            pltpu.VMEM((N_HEADS, HEAD_DIM + 16, TILE), _F32)]


def _fill_padded_queries(qt_ref, qz_ref):
    zeros = jnp.zeros((HEAD_DIM, TILE), _BF16)
    for h in range(N_HEADS):
        lo = (h % 2) * HEAD_DIM
        qz_ref[h, lo:lo + HEAD_DIM, :] = qt_ref[0, h * HEAD_DIM:(h + 1) * HEAD_DIM, :]
        qz_ref[h, HEAD_DIM - lo:2 * HEAD_DIM - lo, :] = zeros


def _dsa_kernel(qt_ref, qit_ref, wit_ref, k_ref, ki2_ref, vt_ref, bias_ref, o_ref,
                hi_ref, lo_ref, qz_ref, ot_ref, jstar_ref, s_ref, p_ref, al_ref, m_ref, acc_ref, *, first_block, topk):
    j = pl.program_id(1) + first_block
    n_tiles = j + 1

    _fill_padded_queries(qt_ref, qz_ref)

    row_chunk = lax.broadcasted_iota(jnp.int32, (TILE, TILE), 0) // CHUNK
    col_chunk = lax.broadcasted_iota(jnp.int32, (TILE, TILE), 1) // CHUNK
    diag_adm = row_chunk <= col_chunk

    def rows_of(kt):
        return pl.ds(pl.multiple_of(kt * TILE, TILE), TILE)

    def score_tile(kt, diagonal):
        acc = jnp.zeros((TILE, TILE), _F32)
        for h in range(N_HEADS):
            kslab = ki2_ref[rows_of(kt), (h % 2) * LANES:(h % 2 + 1) * LANES]
            rel = jnp.dot(kslab, qit_ref[0, (h // 2) * LANES:(h // 2 + 1) * LANES, :],
                          preferred_element_type=_F32)
            acc = acc + wit_ref[0, h:h + 1, :] * jnp.maximum(rel, 0.0)
        acc = jnp.where(acc == 0.0, 0.0, acc)
        if diagonal:
            acc = jnp.where(diag_adm, acc, NEG_INF)
        bits = pltpu.bitcast(acc, jnp.int32)
        key = bits ^ ((bits >> 31) & 0x7FFFFFFF)
        hi_ref[rows_of(kt), :] = (key >> 16).astype(jnp.int16)
        lo_ref[rows_of(kt), :] = ((key & 0xFFFF) - HALF).astype(jnp.int16)

    def score_body(kt, carry):
        score_tile(kt, False)
        return carry

    lax.fori_loop(0, j, score_body, 0)
    score_tile(j, True)

    one, zero = jnp.ones((), _MASK16), jnp.zeros((), _MASK16)
    neg = jnp.full((), NEG_INF, _MASK16)

    def count(indicator):
        def body(kt, cnt):
            ind = indicator(kt)
            parts = [ind[r * 16:(r + 1) * 16] for r in range(TILE // 16)]
            while len(parts) > 1:
                parts = [parts[i] + parts[i + 1] for i in range(0, len(parts), 2)]
            return cnt + parts[0].astype(_F32)
        cnt16 = lax.fori_loop(0, n_tiles, body, jnp.zeros((16, TILE), _F32))
        return jnp.sum(cnt16, axis=0, keepdims=True)

    def to16(v):
        return (v - HALF).astype(jnp.int16)

    def search16(ref, target):
        def bit_body(i, t):
            cand = t | lax.shift_left(jnp.int32(1), 15 - i)
            c16 = to16(cand)
            cnt = count(lambda kt: jnp.where(ref[rows_of(kt), :] >= c16, one, zero))
            return jnp.where(cnt >= target, cand, t)
        return lax.fori_loop(0, 16, bit_body, jnp.zeros((1, TILE), jnp.int32))

    thr_hi = to16(search16(hi_ref, jnp.float32(topk)))
    n_hi_gt = count(lambda kt: jnp.where(hi_ref[rows_of(kt), :] > thr_hi, one, zero))

    lowest = jnp.full((), -HALF, jnp.int16)

    def restrict_body(kt, carry):
        lo_ref[rows_of(kt), :] = jnp.where(hi_ref[rows_of(kt), :] == thr_hi, lo_ref[rows_of(kt), :], lowest)
        return carry

    lax.fori_loop(0, n_tiles, restrict_body, 0)
    thr_lo = to16(search16(lo_ref, topk - n_hi_gt))

    def tied(kt, yes, no):
        return jnp.where(hi_ref[rows_of(kt), :] == thr_hi,
                         jnp.where(lo_ref[rows_of(kt), :] == thr_lo, yes, no), no)

    n_gt = n_hi_gt + count(lambda kt: jnp.where(lo_ref[rows_of(kt), :] > thr_lo, one, zero))
    n_eq = count(lambda kt: tied(kt, one, zero))
    need = topk - n_gt

    big = jnp.int32(2 ** 15 - 1)
    jstar_ref[...] = jnp.full((1, TILE), big, jnp.int32)
    excess = jnp.max(jnp.where(n_eq > need, 1, 0)) > 0
    row_iota = lax.broadcasted_iota(jnp.int32, (TILE, TILE), 0).astype(jnp.int16)

    def index_of(kt):
        return row_iota + (kt * TILE).astype(jnp.int16)

    @pl.when(excess)
    def _():
        def idx_body(i, j0):
            cand = j0 | lax.shift_left(jnp.int32(1), 14 - i)
            c16 = cand.astype(jnp.int16)
            cnt = count(lambda kt: tied(kt, jnp.where(index_of(kt) < c16, one, zero), zero))
            return jnp.where(cnt < need, cand, j0)
        j0 = lax.fori_loop(0, 15, idx_body, jnp.zeros((1, TILE), jnp.int32))
        jstar_ref[...] = jnp.where(n_eq > need, j0 + 1, big)

    jstar = jstar_ref[...].astype(jnp.int16)

    def mask_body(kt, carry):
        hi, lo = hi_ref[rows_of(kt), :], lo_ref[rows_of(kt), :]
        tie = jnp.where(lo == thr_lo, jnp.where(index_of(kt) < jstar, zero, neg), neg)
        same_hi = jnp.where(lo > thr_lo, zero, tie)
        mask = jnp.where(hi > thr_hi, zero, jnp.where(hi == thr_hi, same_hi, neg))
        hi_ref[rows_of(kt), :] = pltpu.bitcast(mask, jnp.int16)
        return carry

    lax.fori_loop(0, n_tiles, mask_body, 0)

    _attend_init(m_ref, acc_ref)

    def attend(kt, near):
        mask = pltpu.bitcast(hi_ref[rows_of(kt), :], _MASK16).astype(_F32)
        add_of = (lambda h: mask) if near is None else (lambda h: mask + bias_ref[near, h])
        _attend_tile(lambda h: k_ref[rows_of(kt), (h // 2) * LANES:(h // 2 + 1) * LANES], qz_ref, add_of,
                     lambda h: vt_ref[kt, h * HEAD_DIM:(h + 1) * HEAD_DIM, :], s_ref, p_ref, al_ref, m_ref, acc_ref)

    def far_body(kt, carry):
        attend(kt, None)
        return carry

    lax.fori_loop(0, jnp.maximum(j - 1, 0), far_body, 0)

    @pl.when(j >= 1)
    def _():
        attend(j - 1, 1)

    attend(j, 0)
    _attend_finish(acc_ref, ot_ref, o_ref)


def _dsa(qt, qit, wit, k, ki2, vt, bias, *, batch, q_tiles, first_block, seq_tiles, topk):
    seq = seq_tiles * TILE
    once = pl.Buffered(1)
    qspec = lambda h: pl.BlockSpec((1, h, TILE), lambda b, jj: (b * q_tiles + jj, 0, 0))
    kern = functools.partial(_dsa_kernel, first_block=first_block, topk=topk)
    return pl.pallas_call(
        kern, out_shape=jax.ShapeDtypeStruct((batch * q_tiles * TILE, D_HEADS), _BF16),
        grid=(batch, q_tiles),
        in_specs=[qspec(D_HEADS), qspec(D_HEADS), qspec(N_HEADS),
                  pl.BlockSpec((seq, D_HEADS), lambda b, jj: (b, 0), pipeline_mode=once),
                  pl.BlockSpec((seq, 2 * LANES), lambda b, jj: (b, 0), pipeline_mode=once),
                  pl.BlockSpec((seq_tiles, D_HEADS, TILE), lambda b, jj: (b, 0, 0), pipeline_mode=once),
                  pl.BlockSpec(bias.shape, lambda b, jj: (0, 0, 0, 0), pipeline_mode=once)],
        out_specs=pl.BlockSpec((TILE, D_HEADS), lambda b, jj: (b * q_tiles + jj, 0)),
        scratch_shapes=[pltpu.VMEM((seq, TILE), jnp.int16), pltpu.VMEM((seq, TILE), jnp.int16),
                        pltpu.VMEM((N_HEADS, LANES, TILE), _BF16),
                        pltpu.VMEM((D_HEADS, TILE), _F32),
                        pltpu.VMEM((1, TILE), jnp.int32)] + _attend_scratch(),
        name="dsa", compiler_params=_params(("parallel", "arbitrary")),
    )(qt, qit, wit, k, ki2, vt, bias)


def _band_kernel(qt_ref, k0_ref, k1_ref, k2_ref, v0_ref, v1_ref, v2_ref, bias_ref, o_ref,
                 qz_ref, ot_ref, s_ref, p_ref, al_ref, m_ref, acc_ref, *, first_block):
    j = pl.program_id(1) + first_block
    _fill_padded_queries(qt_ref, qz_ref)
    _attend_init(m_ref, acc_ref)
    k_refs = (k0_ref, k1_ref, k2_ref)
    v_refs = (v0_ref, v1_ref, v2_ref)
    for t in range(3):
        off = jnp.where(j - 2 + t >= 0, 0.0, NEG_INF).astype(_F32)
        _attend_tile(lambda h: k_refs[t][:, (h // 2) * LANES:(h // 2 + 1) * LANES], qz_ref,
                     lambda h: bias_ref[t, h] + off,
                     lambda h: v_refs[t][0, h * HEAD_DIM:(h + 1) * HEAD_DIM, :], s_ref, p_ref, al_ref, m_ref, acc_ref)
    _attend_finish(acc_ref, ot_ref, o_ref)


def _band(qt, k, vt, bias, *, batch, q_tiles, first_block, seq_tiles):
    def kspec(t):
        return pl.BlockSpec((TILE, D_HEADS),
                            lambda b, jj: (b * seq_tiles + jnp.maximum(jj + first_block - 2 + t, 0), 0))

    def vspec(t):
        return pl.BlockSpec((1, D_HEADS, TILE),
                            lambda b, jj: (b * seq_tiles + jnp.maximum(jj + first_block - 2 + t, 0), 0, 0))

    kern = functools.partial(_band_kernel, first_block=first_block)
    return pl.pallas_call(
        kern, out_shape=jax.ShapeDtypeStruct((batch * q_tiles * TILE, D_HEADS), _BF16),
        grid=(batch, q_tiles),
        in_specs=[pl.BlockSpec((1, D_HEADS, TILE), lambda b, jj: (b * q_tiles + jj, 0, 0)),
                  kspec(0), kspec(1), kspec(2), vspec(0), vspec(1), vspec(2),
                  pl.BlockSpec(bias.shape, lambda b, jj: (0, 0, 0, 0))],
        out_specs=pl.BlockSpec((TILE, D_HEADS), lambda b, jj: (b * q_tiles + jj, 0)),
        scratch_shapes=[pltpu.VMEM((N_HEADS, LANES, TILE), _BF16), pltpu.VMEM((D_HEADS, TILE), _F32)]
        + _attend_scratch(),
        name="band", compiler_params=_params(("parallel", "arbitrary")),
    )(qt, k, k, k, vt, vt, vt, bias)


def _post_attn_kernel(oa_ref, ob_ref, x_ref, woa_ref, wob_ref, g_ref, b_ref, wr_ref, x1_ref, route_ref):
    h = jnp.dot(oa_ref[...], woa_ref[...], preferred_element_type=_F32)
    h = h + jnp.dot(ob_ref[...], wob_ref[...], preferred_element_type=_F32)
    x1 = _layer_norm_rows(DEEPNORM_ALPHA * x_ref[...] + h, g_ref[...], b_ref[...])
    x1_ref[...] = x1
    logits = jnp.dot(x1.astype(_BF16), wr_ref[...], preferred_element_type=_F32)
    lane = lax.broadcasted_iota(jnp.int32, logits.shape, 1)
    lane_f = lane.astype(_F32)
    far = jnp.float32(1e9)

    def first_max(mask):
        v = jnp.max(jnp.where(mask, logits, -jnp.inf), axis=-1, keepdims=True)
        i = jnp.min(jnp.where(mask & (logits == v), lane_f, far), axis=-1, keepdims=True)
        return v, i

    gmask = lane < N_GROUPS
    gmax, gsel = first_max(gmask)
    gsum = jnp.sum(jnp.where(gmask, jnp.exp(logits - gmax), 0.0), axis=-1, keepdims=True)
    g_gate = 1.0 / gsum
    lo = N_GROUPS + gsel * EXPERTS_PER_GROUP
    emask = (lane_f >= lo) & (lane_f < lo + EXPERTS_PER_GROUP)
    v1, i1 = first_max(emask)
    v2, i2 = first_max(emask & (lane_f != i1))
    e2 = jnp.exp(v2 - v1)
    den = 1.0 + e2
    gate1 = (1.0 / den) * g_gate
    gate2 = (e2 / den) * g_gate
    route = jnp.where(lane == 0, i1 - N_GROUPS,
                      jnp.where(lane == 1, i2 - N_GROUPS,
                                jnp.where(lane == 2, gate1, jnp.where(lane == 3, gate2, 0.0))))
    route_ref[...] = route


def _post_attn(oa, ob, x, woa, wob, g, b, wr):
    n = x.shape[0]
    rows = PROJ_ROWS
    row_spec = lambda w: pl.BlockSpec((rows, w), lambda i: (i, 0))
    full = lambda a: pl.BlockSpec(a.shape, lambda i: (0,) * a.ndim)
    return pl.pallas_call(
        _post_attn_kernel,
        out_shape=(jax.ShapeDtypeStruct((n, D_MODEL), _F32), jax.ShapeDtypeStruct((n, LANES), _F32)),
        grid=(n // rows,),
        in_specs=[row_spec(D_HEADS), row_spec(D_HEADS), row_spec(D_MODEL), full(woa), full(wob), full(g), full(b),
                  full(wr)],
        out_specs=(row_spec(D_MODEL), row_spec(LANES)),
        name="post_attn", compiler_params=_params(("parallel",)),
    )(oa, ob, x, woa, wob, g, b, wr)


def _expert_kernel(blk_e_ref, tok_ref, dst_ref, gate_ref, wg_ref, wu_ref, wd_ref, x_hbm, y_hbm,
                   xbuf, ybuf, gsem, ssem):
    del blk_e_ref

    def gather(r):
        return pltpu.make_async_copy(x_hbm.at[pl.ds(tok_ref[0, 0, r], 1)], xbuf.at[pl.ds(r, 1)], gsem)

    def scatter(r):
        return pltpu.make_async_copy(ybuf.at[pl.ds(r, 1)], y_hbm.at[pl.ds(dst_ref[0, 0, r], 1)], ssem)

    def start_gather(r, c):
        gather(r).start()
        return c

    def wait_gather(r, c):
        gather(r).wait()
        return c

    lax.fori_loop(0, EXPERT_ROWS, start_gather, 0)
    lax.fori_loop(0, EXPERT_ROWS, wait_gather, 0)

    xb = xbuf[...].astype(_BF16)
    gpre = jnp.dot(xb, wg_ref[0], preferred_element_type=_F32)
    up = jnp.dot(xb, wu_ref[0], preferred_element_type=_F32)
    hmid = (gpre * (1.0 / (1.0 + jnp.exp(-gpre)))) * up
    y = jnp.dot(hmid.astype(_BF16), wd_ref[0], preferred_element_type=_F32)
    ybuf[...] = y * gate_ref[...]

    def start_scatter(r, c):
        scatter(r).start()
        return c

    def wait_scatter(r, c):
        scatter(r).wait()
        return c

    lax.fori_loop(0, EXPERT_ROWS, start_scatter, 0)
    lax.fori_loop(0, EXPERT_ROWS, wait_scatter, 0)


def _expert_mlp(blk_e, buf_tok, buf_dst, buf_gate, wg, wu, wd, x1, n_out_rows):
    nb = blk_e.shape[0]
    rows = EXPERT_ROWS
    smem_ids = pl.BlockSpec((1, 1, rows), lambda i, e: (i, 0, 0), memory_space=pltpu.SMEM)
    wspec = lambda a: pl.BlockSpec((1,) + a.shape[1:], lambda i, e: (e[i], 0, 0))
    grid_spec = pltpu.PrefetchScalarGridSpec(
        num_scalar_prefetch=1, grid=(nb,),
        in_specs=[smem_ids, smem_ids, pl.BlockSpec((rows, 1), lambda i, e: (i, 0)),
                  wspec(wg), wspec(wu), wspec(wd), pl.BlockSpec(memory_space=pl.ANY)],
        out_specs=pl.BlockSpec(memory_space=pl.ANY),
        scratch_shapes=[pltpu.VMEM((rows, D_MODEL), _F32), pltpu.VMEM((rows, D_MODEL), _F32),
                        pltpu.SemaphoreType.DMA(()), pltpu.SemaphoreType.DMA(())])
    return pl.pallas_call(
        _expert_kernel, out_shape=jax.ShapeDtypeStruct((n_out_rows, D_MODEL), _F32), grid_spec=grid_spec,
        name="expert_mlp", compiler_params=_params(("arbitrary",)),
    )(blk_e, buf_tok.reshape(nb, 1, rows), buf_dst.reshape(nb, 1, rows), buf_gate.reshape(nb * rows, 1),
      wg, wu, wd, x1)


def _dispatch(eid, gate):
    n = eid.shape[0]
    m = 2 * n
    rows = EXPERT_ROWS
    eid_f = eid.reshape(m)
    onehot = (eid_f[:, None] == jnp.arange(N_EXPERTS, dtype=jnp.int32)[None, :]).astype(jnp.int32)
    csum = jnp.cumsum(onehot, axis=0)
    rank = jnp.sum(csum * onehot, axis=1) - 1
    counts = csum[-1]
    padded = (counts + rows - 1) // rows * rows
    pend = jnp.cumsum(padded)
    pstart = pend - padded
    slot = pstart[eid_f] + rank
    n_blocks = -(-m // rows) + N_EXPERTS
    n_slots = n_blocks * rows
    spare = m + jnp.arange(n_slots, dtype=jnp.int32) % rows
    buf_tok = jnp.zeros((n_slots,), jnp.int32).at[slot].set(jnp.arange(m, dtype=jnp.int32) // 2)
    a = jnp.arange(m, dtype=jnp.int32)
    buf_dst = spare.at[slot].set((a % 2) * n + a // 2)
    buf_gate = jnp.zeros((n_slots,), _F32).at[slot].set(gate.reshape(m))
    blk_e = jnp.minimum(jnp.searchsorted(pend, jnp.arange(n_blocks, dtype=jnp.int32) * rows, side='right'),
                        N_EXPERTS - 1).astype(jnp.int32)
    return blk_e, buf_tok, buf_dst, buf_gate


def _combine_kernel(x1_ref, ya_ref, yb_ref, g_ref, b_ref, o_ref):
    moe = ya_ref[...] + yb_ref[...]
    o_ref[...] = _layer_norm_rows(DEEPNORM_ALPHA * x1_ref[...] + moe, g_ref[...], b_ref[...])


def _combine(x1, y2, g, b):
    n = x1.shape[0]
    rows = PROJ_ROWS
    steps = n // rows
    full = lambda a: pl.BlockSpec(a.shape, lambda i: (0,) * a.ndim)
    return pl.pallas_call(
        _combine_kernel, out_shape=jax.ShapeDtypeStruct((n, D_MODEL), _F32), grid=(steps,),
        in_specs=[pl.BlockSpec((rows, D_MODEL), lambda i: (i, 0)), pl.BlockSpec((rows, D_MODEL), lambda i: (i, 0)),
                  pl.BlockSpec((rows, D_MODEL), lambda i: (steps + i, 0)), full(g), full(b)],
        out_specs=pl.BlockSpec((rows, D_MODEL), lambda i: (i, 0)),
        name="combine", compiler_params=_params(("parallel",)),
    )(x1, y2, y2, g, b)


def _t5_bucket(rel):
    half = T5_BUCKETS // 2
    max_exact = half // 2
    ret = jnp.where(rel > 0, half, 0)
    n = jnp.abs(rel)
    nf = jnp.maximum(n, 1).astype(jnp.float32)
    large = max_exact + (jnp.log(nf / max_exact) / math.log(T5_MAX_DIST / max_exact)
                         * (half - max_exact)).astype(jnp.int32)
    large = jnp.minimum(large, half - 1)
    return ret + jnp.where(n < max_exact, n, large)


def _toeplitz(f):
    period = 2 * TILE
    f = jnp.concatenate([f, jnp.zeros(f.shape[:-1] + (1,), f.dtype)], axis=-1)
    rep = jnp.tile(f, TILE)[..., :TILE * (period - 1)].reshape(f.shape[:-1] + (TILE, period - 1))
    return rep[..., TILE - 1:]


def _dsa_bias_tiles(t5_table):
    span = 2 * TILE - 1
    rel = jnp.arange(3 * TILE - 1, dtype=jnp.int32) - span
    far = t5_table[_t5_bucket(jnp.int32(-2 * TILE))]
    by_rel = ((t5_table[_t5_bucket(rel)] - far) * LOG2E).T
    tiles = [jnp.swapaxes(_toeplitz(by_rel[:, TILE * (1 - d):TILE * (1 - d) + span]), -1, -2) for d in range(2)]
    r = jnp.arange(TILE, dtype=jnp.int32)[:, None]
    c = jnp.arange(TILE, dtype=jnp.int32)[None, :]
    tiles[0] = jnp.where(r // CHUNK <= c // CHUNK, tiles[0], NEG_INF)
    return jnp.stack(tiles).astype(_F32)


def _band_bias_tiles(rel_b):
    span = 2 * TILE - 1
    dist = jnp.arange(4 * TILE - 1, dtype=jnp.int32) - (TILE - 1)
    by_dist = (rel_b[jnp.clip(dist, -REL_CLIP, REL_CLIP) + REL_CLIP] * LOG2E).T
    tiles = jnp.stack([_toeplitz(by_dist[:, TILE * (2 - t):TILE * (2 - t) + span]) for t in range(3)])
    t = jnp.arange(3, dtype=jnp.int32)[:, None, None]
    r = jnp.arange(TILE, dtype=jnp.int32)[None, :, None]
    c = jnp.arange(TILE, dtype=jnp.int32)[None, None, :]
    kc = CHUNKS_PER_TILE * (t - 2) + r // CHUNK
    qc = c // CHUNK
    vis = (kc <= qc) & (kc >= qc - BAND_CHUNKS)
    return jnp.where(vis[:, None], tiles, NEG_INF).astype(_F32)


def _moe_and_norm(x1, route, wg, wu, wd, g2, b2):
    n = x1.shape[0]
    eid = route[:, 0:2].astype(jnp.int32)
    gate = route[:, 2:4]
    blk_e, buf_tok, buf_dst, buf_gate = _dispatch(eid, gate)
    y2 = _expert_mlp(blk_e, buf_tok, buf_dst, buf_gate, wg, wu, wd, x1, 2 * n + EXPERT_ROWS)
    return _combine(x1, y2, g2, b2)


def kernel(x_prompt, x_sample, cache_a_k, cache_a_v, cache_a_kidx, cache_b_k, cache_b_v, t5_table, w_in, ln_idx_g,
           ln_idx_b, rel_b, w_o, ln1_g, ln1_b, w_group, w_expert, w_gate, w_up, w_down, ln2_g, ln2_b):
    assert w_in.shape[0] == DEPTH
    B, T, _ = x_prompt.shape
    S, TS, _ = x_sample.shape
    P = cache_a_k.shape[2]
    W = cache_b_k.shape[2]
    assert T % PROJ_ROWS == 0 and (S * TS) % PROJ_ROWS == 0 and P % TILE == 0 and TS <= CHUNK and W == BAND_ROWS

    w = w_in[0]
    cuts = np.cumsum([0, D_HEADS, D_HEADS, D_HEADS, N_HEADS * D_IDX, D_IDX, N_HEADS, D_HEADS, D_HEADS, D_HEADS])
    w_qa, w_ka, w_va, w_qi, w_ki, w_wi, w_qb, w_kb, w_vb = [w[:, cuts[i]:cuts[i + 1]] for i in range(9)]
    wnat = jnp.concatenate([w_ka, w_va, w_kb, w_vb], axis=1).astype(_BF16)
    wsm = jnp.pad(w_ki, ((0, 0), (0, LANES - D_IDX))).astype(_BF16)
    wt = jnp.concatenate([w_qa, w_va, w_qi, w_qb, w_vb], axis=1).T.astype(_BF16)
    wwi = jnp.pad(w_wi.T, ((0, 16 - N_HEADS), (0, 0))).astype(_BF16)
    g_idx = jnp.pad(ln_idx_g[0], (0, LANES - D_IDX)).reshape(1, LANES)
    b_idx = jnp.pad(ln_idx_b[0], (0, LANES - D_IDX)).reshape(1, LANES)
    woa = w_o[0, :D_HEADS].astype(_BF16)
    wob = w_o[0, D_HEADS:].astype(_BF16)
    wr = jnp.pad(jnp.concatenate([w_group[0], w_expert[0]], axis=1),
                 ((0, 0), (0, LANES - N_GROUPS - N_EXPERTS))).astype(_BF16)
    wg, wu, wd = w_gate[0].astype(_BF16), w_up[0].astype(_BF16), w_down[0].astype(_BF16)
    g1, b1 = ln1_g[0].reshape(1, D_MODEL), ln1_b[0].reshape(1, D_MODEL)
    g2, b2 = ln2_g[0].reshape(1, D_MODEL), ln2_b[0].reshape(1, D_MODEL)
    bias_a = _dsa_bias_tiles(t5_table)
    bias_b = _band_bias_tiles(rel_b[0])

    def finish(x_flat, oa, ob):
        x1, route = _post_attn(oa, ob, x_flat, woa, wob, g1, b1, wr)
        return _moe_and_norm(x1, route, wg, wu, wd, g2, b2)

    xp = x_prompt.reshape(B * T, D_MODEL)
    (ka, va, kb, vb, ki, kab, kbb, ki2, qat, vat, qit, qbt, vbt, wit) = _project(xp, wnat, wsm, wt, wwi, g_idx, b_idx)
    tiles = T // TILE
    oa = _dsa(qat, qit, wit, kab, ki2, vat, bias_a, batch=B, q_tiles=tiles, first_block=0, seq_tiles=tiles,
              topk=min(TOPK_MAX, T // 4))
    ob = _band(qbt, kbb, vbt, bias_b, batch=B, q_tiles=tiles, first_block=0, seq_tiles=tiles)
    y_prompt = finish(xp, oa, ob).reshape(B, T, D_MODEL)
    keep = min(BAND_ROWS, T)
    heads = lambda a, b_, t_: a.reshape(1, b_, t_, N_HEADS, HEAD_DIM)
    st_p = (heads(ka, B, T), heads(va, B, T), ki.reshape(1, B, T, D_IDX),
            heads(kb, B, T)[:, :, T - keep:], heads(vb, B, T)[:, :, T - keep:])

    xs = x_sample.reshape(S * TS, D_MODEL)
    (ka, va, kb, vb, ki, kab, kbb, ki2, qat, vat, qit, qbt, vbt, wit) = _project(xs, wnat, wsm, wt, wwi, g_idx, b_idx)
    seq_tiles = P // TILE + 1
    pad_rows = seq_tiles * TILE - P - TS

    def seq_nat(cache, new, width):
        parts = [cache.reshape(S, -1, width).astype(_BF16), new.reshape(S, TS, width).astype(_BF16),
                 jnp.zeros((S, pad_rows, width), _BF16)]
        return jnp.concatenate(parts, axis=1)

    def seq_tr(nat):
        return jnp.transpose(nat.reshape(S, seq_tiles, TILE, -1), (0, 1, 3, 2)).reshape(S * seq_tiles, -1, TILE)

    def q_tiles_of(tr, rows):
        flat = jnp.transpose(tr, (1, 0, 2)).reshape(rows, S, TS)
        return jnp.transpose(jnp.pad(flat, ((0, 0), (0, 0), (0, TILE - TS))), (1, 0, 2))

    cki = cache_a_kidx[0]
    zpad = jnp.zeros(cki.shape[:-1] + (LANES - D_IDX,), cki.dtype)
    cache_ki2 = jnp.concatenate([cki, zpad, zpad, cki], axis=-1)
    k_seq = seq_nat(cache_a_k[0], kab, D_HEADS)
    v_seq = seq_nat(cache_a_v[0], va, D_HEADS)
    ki2_seq = seq_nat(cache_ki2, ki2, 2 * LANES)
    L = P + TS
    oa = _dsa(q_tiles_of(qat, D_HEADS), q_tiles_of(qit, D_HEADS), q_tiles_of(wit, N_HEADS),
              k_seq.reshape(-1, D_HEADS), ki2_seq.reshape(-1, 2 * LANES), seq_tr(v_seq), bias_a,
              batch=S, q_tiles=1, first_block=P // TILE, seq_tiles=seq_tiles, topk=min(TOPK_MAX, L // 4))
    zeros_front = jnp.zeros((S, P - W, D_HEADS), _BF16)
    kb_seq = jnp.concatenate([zeros_front, seq_nat(cache_b_k[0], kbb, D_HEADS)], axis=1)
    vb_seq = jnp.concatenate([zeros_front, seq_nat(cache_b_v[0], vb, D_HEADS)], axis=1)
    ob = _band(q_tiles_of(qbt, D_HEADS), kb_seq.reshape(-1, D_HEADS), seq_tr(vb_seq), bias_b,
               batch=S, q_tiles=1, first_block=P // TILE, seq_tiles=seq_tiles)
    take = lambda o: o.reshape(S, TILE, D_HEADS)[:, :TS].reshape(S * TS, D_HEADS)
    y_sample = finish(xs, take(oa), take(ob)).reshape(S, TS, D_MODEL)
    band_k = jnp.concatenate([cache_b_k[0], heads(kb, S, TS)[0]], axis=1)[:, TS:]
    band_v = jnp.concatenate([cache_b_v[0], heads(vb, S, TS)[0]], axis=1)[:, TS:]
    st_s = (heads(ka, S, TS), heads(va, S, TS), ki.reshape(1, S, TS, D_IDX), band_k[None], band_v[None])

    return (y_prompt, y_sample) + st_p + st_s
```

```python
import functools
import math

import numpy as np
import jax
import jax.numpy as jnp
from jax import lax
from jax.experimental import pallas as pl
from jax.experimental.pallas import tpu as pltpu

D_MODEL = 1024
CHUNK = 64
HEAD_DIM = 64
N_HEADS = 8
D_HEADS = N_HEADS * HEAD_DIM
D_IDX = 64
TOPK_MAX = 256
T5_BUCKETS = 32
T5_MAX_DIST = 128
BAND_CHUNKS = 8
BAND_ROWS = BAND_CHUNKS * CHUNK
REL_CLIP = 128
N_GROUPS = 4
EXPERTS_PER_GROUP = 8
N_EXPERTS = N_GROUPS * EXPERTS_PER_GROUP
D_EXPERT = 512
DEPTH = 1
DEEPNORM_ALPHA = (2 * DEPTH) ** 0.25
LN_EPS = 1e-5
NEG_INF = -1e30
IDX_SCALE = (N_HEADS * D_IDX) ** -0.5
LOG2E = math.log2(math.e)
Q_SCALE = HEAD_DIM ** -0.5 * LOG2E

TILE = 256
CHUNKS_PER_TILE = TILE // CHUNK
LANES = 128
ROW_TILE = D_MODEL // LANES
PROJ_ROWS = 512
EXPERT_ROWS = 256
COUNT_GROUP = 4
V7X_VMEM_LIMIT = 56 * 1024 * 1024

_BF16 = jnp.bfloat16
_MASK16 = jnp.bfloat16
_F32 = jnp.float32


HALF = 2 ** 15


def _params(sem, vmem=V7X_VMEM_LIMIT):
    return pltpu.CompilerParams(dimension_semantics=sem, vmem_limit_bytes=vmem)


def _to_row_tiles(ref, x):
    rows = x.shape[0]
    for s in range(ROW_TILE):
        ref[pl.ds(s, rows, stride=ROW_TILE), :] = x[:, s * LANES:(s + 1) * LANES]


def _from_row_tiles(ref):
    rows = ref.shape[0] // ROW_TILE
    return jnp.concatenate([ref[pl.ds(s, rows, stride=ROW_TILE), :] for s in range(ROW_TILE)], axis=1)


def _layer_norm_rows(x, g, b):
    mu = jnp.mean(x, axis=-1, keepdims=True)
    xc = x - mu
    var = jnp.mean(xc * xc, axis=-1, keepdims=True)
    return xc * lax.rsqrt(var + LN_EPS) * g + b


def _project_kernel(x_ref, wnat_ref, wsm_ref, wt_ref, wwi_ref, g_ref, b_ref,
                    ka_ref, va_ref, kb_ref, vb_ref, ki_ref,
                    kab_ref, kbb_ref, ki2_ref,
                    qat_ref, vat_ref, qit_ref, qbt_ref, vbt_ref, wit_ref):
    x = x_ref[...]
    xb = x.astype(_BF16)
    xtb = x.T.astype(_BF16)

    def nat(i):
        return jnp.dot(xb, wnat_ref[:, i * D_HEADS:(i + 1) * D_HEADS], preferred_element_type=_F32)

    ka = nat(0)
    ka_ref[...] = ka
    kab_ref[...] = ka.astype(_BF16)
    va_ref[...] = nat(1)
    kb = nat(2)
    kb_ref[...] = kb
    kbb_ref[...] = kb.astype(_BF16)
    vb_ref[...] = nat(3)

    zs = jnp.dot(xb, wsm_ref[...], preferred_element_type=_F32)
    lane = lax.broadcasted_iota(jnp.int32, zs.shape, 1)
    mu = jnp.sum(zs, axis=-1, keepdims=True) * (1.0 / D_IDX)
    zc = jnp.where(lane < D_IDX, zs - mu, 0.0)
    var = jnp.sum(zc * zc, axis=-1, keepdims=True) * (1.0 / D_IDX)
    ln = zc * lax.rsqrt(var + LN_EPS) * g_ref[...] + b_ref[...]
    ki_ref[...] = ln[:, :D_IDX]
    ki2_ref[:, :LANES] = ln.astype(_BF16)
    ki2_ref[:, LANES:] = pltpu.roll(ln, D_IDX, axis=1).astype(_BF16)

    def tr(i, scale):
        z = jnp.dot(wt_ref[i * D_HEADS:(i + 1) * D_HEADS, :], xtb, preferred_element_type=_F32)
        if scale != 1.0:
            z = z * scale
        return z.astype(_BF16)

    def put(ref, z):
        for t in range(ref.shape[0]):
            ref[t] = z[:, t * TILE:(t + 1) * TILE]

    put(qat_ref, tr(0, Q_SCALE))
    put(vat_ref, tr(1, 1.0))
    put(qit_ref, tr(2, 1.0))
    put(qbt_ref, tr(3, Q_SCALE))
    put(vbt_ref, tr(4, 1.0))
    wi = jnp.dot(wwi_ref[...], xtb, preferred_element_type=_F32) * IDX_SCALE
    put(wit_ref, wi[:N_HEADS])


def _project(x, wnat, wsm, wt, wwi, g, b):
    n = x.shape[0]
    rows = PROJ_ROWS
    tiles = rows // TILE
    nt = n // TILE
    grid = (n // rows,)
    row_spec = lambda w: pl.BlockSpec((rows, w), lambda i: (i, 0))
    full = lambda a: pl.BlockSpec(a.shape, lambda i: (0,) * a.ndim)
    tspec = lambda h: pl.BlockSpec((tiles, h, TILE), lambda i: (i, 0, 0))
    nat_f32 = jax.ShapeDtypeStruct((n, D_HEADS), _F32)
    nat_b16 = jax.ShapeDtypeStruct((n, D_HEADS), _BF16)
    tr_b16 = jax.ShapeDtypeStruct((nt, D_HEADS, TILE), _BF16)
    out_shape = (nat_f32, nat_f32, nat_f32, nat_f32, jax.ShapeDtypeStruct((n, D_IDX), _F32),
                 nat_b16, nat_b16, jax.ShapeDtypeStruct((n, 2 * LANES), _BF16),
                 tr_b16, tr_b16, tr_b16, tr_b16, tr_b16,
                 jax.ShapeDtypeStruct((nt, N_HEADS, TILE), _F32))
    out_specs = (row_spec(D_HEADS),) * 4 + (row_spec(D_IDX),) + (row_spec(D_HEADS),) * 2 + (row_spec(2 * LANES),) \
        + (tspec(D_HEADS),) * 5 + (tspec(N_HEADS),)
    return pl.pallas_call(
        _project_kernel, out_shape=out_shape, grid=grid,
        in_specs=[row_spec(D_MODEL), full(wnat), full(wsm), full(wt), full(wwi), full(g), full(b)],
        out_specs=out_specs, name="project",
        compiler_params=_params(("parallel",)),
    )(x, wnat, wsm, wt, wwi, g, b)


def _attend_tile(k_of, qz_ref, add_of, vt_of, s_ref, p_ref, al_ref, m_ref, acc_ref):
    for h in range(N_HEADS):
        s_ref[h] = jnp.dot(k_of(h), qz_ref[h], preferred_element_type=_F32) + add_of(h)
    for h in range(N_HEADS):
        s = s_ref[h]
        m = m_ref[h]
        m_new = jnp.maximum(m, jnp.max(s, axis=0, keepdims=True))
        al_ref[h] = jnp.exp2(m - m_new)
        p_ref[h] = jnp.exp2(s - m_new).astype(_BF16)
        m_ref[h] = m_new
    ones = jnp.ones((16, TILE), _BF16)
    for h in range(N_HEADS):
        lhs = jnp.concatenate([vt_of(h), ones], axis=0)
        acc_ref[h] = al_ref[h] * acc_ref[h] + jnp.dot(lhs, p_ref[h], preferred_element_type=_F32)


def _attend_init(m_ref, acc_ref):
    m_ref[...] = jnp.full(m_ref.shape, NEG_INF, _F32)
    acc_ref[...] = jnp.zeros(acc_ref.shape, _F32)


def _attend_finish(acc_ref, ot_ref, o_ref):
    for h in range(N_HEADS):
        a = acc_ref[h]
        ot_ref[h * HEAD_DIM:(h + 1) * HEAD_DIM, :] = a[:HEAD_DIM] / a[HEAD_DIM:HEAD_DIM + 1]
    o_ref[...] = ot_ref[...].T.astype(_BF16)


def _attend_scratch():
    return [pltpu.VMEM((N_HEADS, TILE, TILE), _F32), pltpu.VMEM((N_HEADS, TILE, TILE), _BF16),
            pltpu.VMEM((N_HEADS, 1, TILE), _F32), pltpu.VMEM((N_HEADS, 1, TILE), _F32),
            pltpu.VMEM((N_HEADS, HEAD_DIM + 16, TILE), _F32)]


def _fill_padded_queries(qt_ref, qz_ref):
    zeros = jnp.zeros((HEAD_DIM, TILE), _BF16)
    for h in range(N_HEADS):
        lo = (h % 2) * HEAD_DIM
        qz_ref[h, lo:lo + HEAD_DIM, :] = qt_ref[0, h * HEAD_DIM:(h + 1) * HEAD_DIM, :]
        qz_ref[h, HEAD_DIM - lo:2 * HEAD_DIM - lo, :] = zeros


def _dsa_kernel(qt_ref, qit_ref, wit_ref, k_ref, ki2_ref, vt_ref, bias_ref, o_ref,
                hi_ref, lo_ref, qz_ref, ot_ref, jstar_ref, s_ref, p_ref, al_ref, m_ref, acc_ref, *, first_block, topk):
    j = pl.program_id(1) + first_block
    n_tiles = j + 1

    _fill_padded_queries(qt_ref, qz_ref)

    row_chunk = lax.broadcasted_iota(jnp.int32, (TILE, TILE), 0) // CHUNK
    col_chunk = lax.broadcasted_iota(jnp.int32, (TILE, TILE), 1) // CHUNK
    diag_adm = row_chunk <= col_chunk

    def rows_of(kt):
        return pl.ds(pl.multiple_of(kt * TILE, TILE), TILE)

    def score_tile(kt, diagonal):
        acc = jnp.zeros((TILE, TILE), _F32)
        for h in range(N_HEADS):
            kslab = ki2_ref[rows_of(kt), (h % 2) * LANES:(h % 2 + 1) * LANES]
            rel = jnp.dot(kslab, qit_ref[0, (h // 2) * LANES:(h // 2 + 1) * LANES, :],
                          preferred_element_type=_F32)
            acc = acc + wit_ref[0, h:h + 1, :] * jnp.maximum(rel, 0.0)
        acc = jnp.where(acc == 0.0, 0.0, acc)
        if diagonal:
            acc = jnp.where(diag_adm, acc, NEG_INF)
        bits = pltpu.bitcast(acc, jnp.int32)
        key = bits ^ ((bits >> 31) & 0x7FFFFFFF)
        hi_ref[rows_of(kt), :] = (key >> 16).astype(jnp.int16)
        lo_ref[rows_of(kt), :] = ((key & 0xFFFF) - HALF).astype(jnp.int16)

    def score_body(kt, carry):
        score_tile(kt, False)
        return carry

    lax.fori_loop(0, j, score_body, 0)
    score_tile(j, True)

    one, zero = jnp.ones((), _MASK16), jnp.zeros((), _MASK16)
    neg = jnp.full((), NEG_INF, _MASK16)

    n_groups = (n_tiles + COUNT_GROUP - 1) // COUNT_GROUP
    lowest = jnp.full((), -HALF, jnp.int16)

    def pad_body(kt, carry):
        hi_ref[rows_of(kt), :] = jnp.full((TILE, TILE), lowest, jnp.int16)
        lo_ref[rows_of(kt), :] = jnp.full((TILE, TILE), lowest, jnp.int16)
        return carry

    lax.fori_loop(n_tiles, n_groups * COUNT_GROUP, pad_body, 0)

    def count(indicator):
        def body(g, cnt):
            parts = []
            for u in range(COUNT_GROUP):
                ind = indicator(g * COUNT_GROUP + u)
                parts += [ind[r * 16:(r + 1) * 16] for r in range(TILE // 16)]
            while len(parts) > 1:
                parts = [parts[i] + parts[i + 1] for i in range(0, len(parts), 2)]
            return cnt + parts[0].astype(_F32)
        cnt16 = lax.fori_loop(0, n_groups, body, jnp.zeros((16, TILE), _F32))
        return jnp.sum(cnt16, axis=0, keepdims=True)

    def to16(v):
        return (v - HALF).astype(jnp.int16)

    def search16(ref, target):
        def bit_body(i, t):
            cand = t | lax.shift_left(jnp.int32(1), 15 - i)
            c16 = to16(cand)
            cnt = count(lambda kt: jnp.where(ref[rows_of(kt), :] >= c16, one, zero))
            return jnp.where(cnt >= target, cand, t)
        return lax.fori_loop(0, 16, bit_body, jnp.zeros((1, TILE), jnp.int32))

    thr_hi = to16(search16(hi_ref, jnp.float32(topk)))
    n_hi_gt = count(lambda kt: jnp.where(hi_ref[rows_of(kt), :] > thr_hi, one, zero))


    def restrict_body(kt, carry):
        lo_ref[rows_of(kt), :] = jnp.where(hi_ref[rows_of(kt), :] == thr_hi, lo_ref[rows_of(kt), :], lowest)
        return carry

    lax.fori_loop(0, n_tiles, restrict_body, 0)
    thr_lo = to16(search16(lo_ref, topk - n_hi_gt))

    def tied(kt, yes, no):
        return jnp.where(hi_ref[rows_of(kt), :] == thr_hi,
                         jnp.where(lo_ref[rows_of(kt), :] == thr_lo, yes, no), no)

    n_gt = n_hi_gt + count(lambda kt: jnp.where(lo_ref[rows_of(kt), :] > thr_lo, one, zero))
    n_eq = count(lambda kt: tied(kt, one, zero))
    need = topk - n_gt

    big = jnp.int32(2 ** 15 - 1)
    jstar_ref[...] = jnp.full((1, TILE), big, jnp.int32)
    excess = jnp.max(jnp.where(n_eq > need, 1, 0)) > 0
    row_iota = lax.broadcasted_iota(jnp.int32, (TILE, TILE), 0).astype(jnp.int16)

    def index_of(kt):
        return row_iota + (kt * TILE).astype(jnp.int16)

    @pl.when(excess)
    def _():
        def idx_body(i, j0):
            cand = j0 | lax.shift_left(jnp.int32(1), 14 - i)
            c16 = cand.astype(jnp.int16)
            cnt = count(lambda kt: tied(kt, jnp.where(index_of(kt) < c16, one, zero), zero))
            return jnp.where(cnt < need, cand, j0)
        j0 = lax.fori_loop(0, 15, idx_body, jnp.zeros((1, TILE), jnp.int32))
        jstar_ref[...] = jnp.where(n_eq > need, j0 + 1, big)

    jstar = jstar_ref[...].astype(jnp.int16)

    def mask_body(kt, carry):
        hi, lo = hi_ref[rows_of(kt), :], lo_ref[rows_of(kt), :]
        tie = jnp.where(lo == thr_lo, jnp.where(index_of(kt) < jstar, zero, neg), neg)
        same_hi = jnp.where(lo > thr_lo, zero, tie)
        mask = jnp.where(hi > thr_hi, zero, jnp.where(hi == thr_hi, same_hi, neg))
        hi_ref[rows_of(kt), :] = pltpu.bitcast(mask, jnp.int16)
        return carry

    lax.fori_loop(0, n_tiles, mask_body, 0)

    _attend_init(m_ref, acc_ref)

    def attend(kt, near):
        mask = pltpu.bitcast(hi_ref[rows_of(kt), :], _MASK16).astype(_F32)
        add_of = (lambda h: mask) if near is None else (lambda h: mask + bias_ref[near, h])
        _attend_tile(lambda h: k_ref[rows_of(kt), (h // 2) * LANES:(h // 2 + 1) * LANES], qz_ref, add_of,
                     lambda h: vt_ref[kt, h * HEAD_DIM:(h + 1) * HEAD_DIM, :], s_ref, p_ref, al_ref, m_ref, acc_ref)

    def far_body(kt, carry):
        attend(kt, None)
        return carry

    lax.fori_loop(0, jnp.maximum(j - 1, 0), far_body, 0)

    @pl.when(j >= 1)
    def _():
        attend(j - 1, 1)

    attend(j, 0)
    _attend_finish(acc_ref, ot_ref, o_ref)


def _dsa(qt, qit, wit, k, ki2, vt, bias, *, batch, q_tiles, first_block, seq_tiles, topk):
    seq = seq_tiles * TILE
    once = pl.Buffered(1)
    qspec = lambda h: pl.BlockSpec((1, h, TILE), lambda b, jj: (b * q_tiles + jj, 0, 0))
    kern = functools.partial(_dsa_kernel, first_block=first_block, topk=topk)
    return pl.pallas_call(
        kern, out_shape=jax.ShapeDtypeStruct((batch * q_tiles * TILE, D_HEADS), _BF16),
        grid=(batch, q_tiles),
        in_specs=[qspec(D_HEADS), qspec(D_HEADS), qspec(N_HEADS),
                  pl.BlockSpec((seq, D_HEADS), lambda b, jj: (b, 0), pipeline_mode=once),
                  pl.BlockSpec((seq, 2 * LANES), lambda b, jj: (b, 0), pipeline_mode=once),
                  pl.BlockSpec((seq_tiles, D_HEADS, TILE), lambda b, jj: (b, 0, 0), pipeline_mode=once),
                  pl.BlockSpec(bias.shape, lambda b, jj: (0, 0, 0, 0), pipeline_mode=once)],
        out_specs=pl.BlockSpec((TILE, D_HEADS), lambda b, jj: (b * q_tiles + jj, 0)),
        scratch_shapes=[pltpu.VMEM((seq + (COUNT_GROUP - 1) * TILE, TILE), jnp.int16)] * 2 + [
                        pltpu.VMEM((N_HEADS, LANES, TILE), _BF16),
                        pltpu.VMEM((D_HEADS, TILE), _F32),
                        pltpu.VMEM((1, TILE), jnp.int32)] + _attend_scratch(),
        name="dsa", compiler_params=_params(("parallel", "arbitrary")),
    )(qt, qit, wit, k, ki2, vt, bias)


def _band_kernel(qt_ref, k0_ref, k1_ref, k2_ref, v0_ref, v1_ref, v2_ref, bias_ref, o_ref,
                 qz_ref, ot_ref, s_ref, p_ref, al_ref, m_ref, acc_ref, *, first_block):
    j = pl.program_id(1) + first_block
    _fill_padded_queries(qt_ref, qz_ref)
    _attend_init(m_ref, acc_ref)
    k_refs = (k0_ref, k1_ref, k2_ref)
    v_refs = (v0_ref, v1_ref, v2_ref)
    for t in range(3):
        off = jnp.where(j - 2 + t >= 0, 0.0, NEG_INF).astype(_F32)
        _attend_tile(lambda h: k_refs[t][:, (h // 2) * LANES:(h // 2 + 1) * LANES], qz_ref,
                     lambda h: bias_ref[t, h] + off,
                     lambda h: v_refs[t][0, h * HEAD_DIM:(h + 1) * HEAD_DIM, :], s_ref, p_ref, al_ref, m_ref, acc_ref)
    _attend_finish(acc_ref, ot_ref, o_ref)


def _band(qt, k, vt, bias, *, batch, q_tiles, first_block, seq_tiles):
    def kspec(t):
        return pl.BlockSpec((TILE, D_HEADS),
                            lambda b, jj: (b * seq_tiles + jnp.maximum(jj + first_block - 2 + t, 0), 0))

    def vspec(t):
        return pl.BlockSpec((1, D_HEADS, TILE),
                            lambda b, jj: (b * seq_tiles + jnp.maximum(jj + first_block - 2 + t, 0), 0, 0))

    kern = functools.partial(_band_kernel, first_block=first_block)
    return pl.pallas_call(
        kern, out_shape=jax.ShapeDtypeStruct((batch * q_tiles * TILE, D_HEADS), _BF16),
        grid=(batch, q_tiles),
        in_specs=[pl.BlockSpec((1, D_HEADS, TILE), lambda b, jj: (b * q_tiles + jj, 0, 0)),
                  kspec(0), kspec(1), kspec(2), vspec(0), vspec(1), vspec(2),
                  pl.BlockSpec(bias.shape, lambda b, jj: (0, 0, 0, 0))],
        out_specs=pl.BlockSpec((TILE, D_HEADS), lambda b, jj: (b * q_tiles + jj, 0)),
        scratch_shapes=[pltpu.VMEM((N_HEADS, LANES, TILE), _BF16), pltpu.VMEM((D_HEADS, TILE), _F32)]
        + _attend_scratch(),
        name="band", compiler_params=_params(("parallel", "arbitrary")),
    )(qt, k, k, k, vt, vt, vt, bias)


def _post_attn_kernel(oa_ref, ob_ref, x_ref, woa_ref, wob_ref, g_ref, b_ref, wr_ref, x1_ref, route_ref):
    h = jnp.dot(oa_ref[...], woa_ref[...], preferred_element_type=_F32)
    h = h + jnp.dot(ob_ref[...], wob_ref[...], preferred_element_type=_F32)
    x1 = _layer_norm_rows(DEEPNORM_ALPHA * x_ref[...] + h, g_ref[...], b_ref[...])
    _to_row_tiles(x1_ref, x1)
    logits = jnp.dot(x1.astype(_BF16), wr_ref[...], preferred_element_type=_F32)
    lane = lax.broadcasted_iota(jnp.int32, logits.shape, 1)
    lane_f = lane.astype(_F32)
    far = jnp.float32(1e9)

    def first_max(mask):
        v = jnp.max(jnp.where(mask, logits, -jnp.inf), axis=-1, keepdims=True)
        i = jnp.min(jnp.where(mask & (logits == v), lane_f, far), axis=-1, keepdims=True)
        return v, i

    gmask = lane < N_GROUPS
    gmax, gsel = first_max(gmask)
    gsum = jnp.sum(jnp.where(gmask, jnp.exp(logits - gmax), 0.0), axis=-1, keepdims=True)
    g_gate = 1.0 / gsum
    lo = N_GROUPS + gsel * EXPERTS_PER_GROUP
    emask = (lane_f >= lo) & (lane_f < lo + EXPERTS_PER_GROUP)
    v1, i1 = first_max(emask)
    v2, i2 = first_max(emask & (lane_f != i1))
    e2 = jnp.exp(v2 - v1)
    den = 1.0 + e2
    gate1 = (1.0 / den) * g_gate
    gate2 = (e2 / den) * g_gate
    route = jnp.where(lane == 0, i1 - N_GROUPS,
                      jnp.where(lane == 1, i2 - N_GROUPS,
                                jnp.where(lane == 2, gate1, jnp.where(lane == 3, gate2, 0.0))))
    route_ref[...] = route


def _post_attn(oa, ob, x, woa, wob, g, b, wr):
    n = x.shape[0]
    rows = PROJ_ROWS
    row_spec = lambda w: pl.BlockSpec((rows, w), lambda i: (i, 0))
    full = lambda a: pl.BlockSpec(a.shape, lambda i: (0,) * a.ndim)
    return pl.pallas_call(
        _post_attn_kernel,
        out_shape=(jax.ShapeDtypeStruct((n * ROW_TILE, LANES), _F32), jax.ShapeDtypeStruct((n, LANES), _F32)),
        grid=(n // rows,),
        in_specs=[row_spec(D_HEADS), row_spec(D_HEADS), row_spec(D_MODEL), full(woa), full(wob), full(g), full(b),
                  full(wr)],
        out_specs=(pl.BlockSpec((rows * ROW_TILE, LANES), lambda i: (i, 0)), row_spec(LANES)),
        name="post_attn", compiler_params=_params(("parallel",)),
    )(oa, ob, x, woa, wob, g, b, wr)


def _expert_kernel(blk_e_ref, tok_ref, tok_next_ref, dst_ref, gate_ref, wg_ref, wu_ref, wd_ref, x_hbm, y_hbm,
                   xbuf, ybuf, gsem, ssem):
    del blk_e_ref
    i = pl.program_id(0)
    nb = pl.num_programs(0)
    slot = i % 2
    rows = EXPERT_ROWS

    def tile_of(r):
        return pl.ds(pl.multiple_of(r * ROW_TILE, ROW_TILE), ROW_TILE)

    def start_gathers(ids_ref, sl):
        def body(r, c):
            pltpu.make_async_copy(x_hbm.at[tile_of(ids_ref[0, 0, r])], xbuf.at[sl, tile_of(r)], gsem.at[sl]).start()
            return c
        lax.fori_loop(0, rows, body, 0, unroll=8)

    def wait_gathers(sl):
        pltpu.make_async_copy(x_hbm.at[pl.ds(0, rows * ROW_TILE)], xbuf.at[sl], gsem.at[sl]).wait()

    def wait_scatters(sl):
        pltpu.make_async_copy(ybuf.at[sl], y_hbm.at[pl.ds(0, rows * ROW_TILE)], ssem.at[sl]).wait()

    @pl.when(i == 0)
    def _():
        start_gathers(tok_ref, 0)

    @pl.when(i + 1 < nb)
    def _():
        start_gathers(tok_next_ref, 1 - slot)

    wait_gathers(slot)

    @pl.when(i >= 2)
    def _():
        wait_scatters(slot)

    xb = jnp.concatenate([xbuf[slot, pl.ds(s, rows, stride=ROW_TILE), :] for s in range(ROW_TILE)],
                         axis=1).astype(_BF16)
    gpre = jnp.dot(xb, wg_ref[0], preferred_element_type=_F32)
    up = jnp.dot(xb, wu_ref[0], preferred_element_type=_F32)
    hmid = (gpre * (1.0 / (1.0 + jnp.exp(-gpre)))) * up
    y = jnp.dot(hmid.astype(_BF16), wd_ref[0], preferred_element_type=_F32) * gate_ref[...]
    for s in range(ROW_TILE):
        ybuf[slot, pl.ds(s, rows, stride=ROW_TILE), :] = y[:, s * LANES:(s + 1) * LANES]

    def scatter_body(r, c):
        pltpu.make_async_copy(ybuf.at[slot, tile_of(r)], y_hbm.at[tile_of(dst_ref[0, 0, r])], ssem.at[slot]).start()
        return c

    lax.fori_loop(0, rows, scatter_body, 0, unroll=8)

    @pl.when(i == nb - 1)
    def _():
        wait_scatters(slot)

        @pl.when(nb >= 2)
        def _():
            wait_scatters(1 - slot)


def _expert_mlp(blk_e, buf_tok, buf_dst, buf_gate, wg, wu, wd, x1t, n_out_rows):
    nb = blk_e.shape[0]
    rows = EXPERT_ROWS
    smem_ids = pl.BlockSpec((1, 1, rows), lambda i, e: (i, 0, 0), memory_space=pltpu.SMEM)
    smem_next = pl.BlockSpec((1, 1, rows), lambda i, e: (jnp.minimum(i + 1, nb - 1), 0, 0), memory_space=pltpu.SMEM)
    wspec = lambda a: pl.BlockSpec((1,) + a.shape[1:], lambda i, e: (e[i], 0, 0))
    buf = pltpu.VMEM((2, rows * ROW_TILE, LANES), _F32)
    grid_spec = pltpu.PrefetchScalarGridSpec(
        num_scalar_prefetch=1, grid=(nb,),
        in_specs=[smem_ids, smem_next, smem_ids, pl.BlockSpec((rows, 1), lambda i, e: (i, 0)),
                  wspec(wg), wspec(wu), wspec(wd), pl.BlockSpec(memory_space=pl.ANY)],
        out_specs=pl.BlockSpec(memory_space=pl.ANY),
        scratch_shapes=[buf, buf, pltpu.SemaphoreType.DMA((2,)), pltpu.SemaphoreType.DMA((2,))])
    ids = buf_tok.reshape(nb, 1, rows)
    return pl.pallas_call(
        _expert_kernel, out_shape=jax.ShapeDtypeStruct((n_out_rows * ROW_TILE, LANES), _F32), grid_spec=grid_spec,
        name="expert_mlp", compiler_params=_params(("arbitrary",)),
    )(blk_e, ids, ids, buf_dst.reshape(nb, 1, rows), buf_gate.reshape(nb * rows, 1), wg, wu, wd, x1t)


def _dispatch(eid, gate):
    n = eid.shape[0]
    m = 2 * n
    rows = EXPERT_ROWS
    eid_f = eid.reshape(m)
    onehot = (eid_f[:, None] == jnp.arange(N_EXPERTS, dtype=jnp.int32)[None, :]).astype(jnp.int32)
    csum = jnp.cumsum(onehot, axis=0)
    rank = jnp.sum(csum * onehot, axis=1) - 1
    counts = csum[-1]
    padded = (counts + rows - 1) // rows * rows
    pend = jnp.cumsum(padded)
    pstart = pend - padded
    slot = pstart[eid_f] + rank
    n_blocks = -(-m // rows) + N_EXPERTS
    n_slots = n_blocks * rows
    spare = m + jnp.arange(n_slots, dtype=jnp.int32) % (2 * rows)
    buf_tok = jnp.zeros((n_slots,), jnp.int32).at[slot].set(jnp.arange(m, dtype=jnp.int32) // 2)
    a = jnp.arange(m, dtype=jnp.int32)
    buf_dst = spare.at[slot].set((a % 2) * n + a // 2)
    buf_gate = jnp.zeros((n_slots,), _F32).at[slot].set(gate.reshape(m))
    blk_e = jnp.minimum(jnp.searchsorted(pend, jnp.arange(n_blocks, dtype=jnp.int32) * rows, side='right'),
                        N_EXPERTS - 1).astype(jnp.int32)
    return blk_e, buf_tok, buf_dst, buf_gate


def _combine_kernel(x1_ref, ya_ref, yb_ref, g_ref, b_ref, o_ref):
    moe = _from_row_tiles(ya_ref) + _from_row_tiles(yb_ref)
    o_ref[...] = _layer_norm_rows(DEEPNORM_ALPHA * _from_row_tiles(x1_ref) + moe, g_ref[...], b_ref[...])


def _combine(x1, y2, g, b):
    n = x1.shape[0] // ROW_TILE
    rows = PROJ_ROWS
    steps = n // rows
    full = lambda a: pl.BlockSpec(a.shape, lambda i: (0,) * a.ndim)
    tiles = lambda off: pl.BlockSpec((rows * ROW_TILE, LANES), lambda i: (off + i, 0))
    return pl.pallas_call(
        _combine_kernel, out_shape=jax.ShapeDtypeStruct((n, D_MODEL), _F32), grid=(steps,),
        in_specs=[tiles(0), tiles(0), tiles(steps), full(g), full(b)],
        out_specs=pl.BlockSpec((rows, D_MODEL), lambda i: (i, 0)),
        name="combine", compiler_params=_params(("parallel",)),
    )(x1, y2, y2, g, b)


def _t5_bucket(rel):
    half = T5_BUCKETS // 2
    max_exact = half // 2
    ret = jnp.where(rel > 0, half, 0)
    n = jnp.abs(rel)
    nf = jnp.maximum(n, 1).astype(jnp.float32)
    large = max_exact + (jnp.log(nf / max_exact) / math.log(T5_MAX_DIST / max_exact)
                         * (half - max_exact)).astype(jnp.int32)
    large = jnp.minimum(large, half - 1)
    return ret + jnp.where(n < max_exact, n, large)


def _toeplitz(f):
    period = 2 * TILE
    f = jnp.concatenate([f, jnp.zeros(f.shape[:-1] + (1,), f.dtype)], axis=-1)
    rep = jnp.tile(f, TILE)[..., :TILE * (period - 1)].reshape(f.shape[:-1] + (TILE, period - 1))
    return rep[..., TILE - 1:]


def _dsa_bias_tiles(t5_table):
    span = 2 * TILE - 1
    rel = jnp.arange(3 * TILE - 1, dtype=jnp.int32) - span
    far = t5_table[_t5_bucket(jnp.int32(-2 * TILE))]
    by_rel = ((t5_table[_t5_bucket(rel)] - far) * LOG2E).T
    tiles = [jnp.swapaxes(_toeplitz(by_rel[:, TILE * (1 - d):TILE * (1 - d) + span]), -1, -2) for d in range(2)]
    r = jnp.arange(TILE, dtype=jnp.int32)[:, None]
    c = jnp.arange(TILE, dtype=jnp.int32)[None, :]
    tiles[0] = jnp.where(r // CHUNK <= c // CHUNK, tiles[0], NEG_INF)
    return jnp.stack(tiles).astype(_F32)


def _band_bias_tiles(rel_b):
    span = 2 * TILE - 1
    dist = jnp.arange(4 * TILE - 1, dtype=jnp.int32) - (TILE - 1)
    by_dist = (rel_b[jnp.clip(dist, -REL_CLIP, REL_CLIP) + REL_CLIP] * LOG2E).T
    tiles = jnp.stack([_toeplitz(by_dist[:, TILE * (2 - t):TILE * (2 - t) + span]) for t in range(3)])
    t = jnp.arange(3, dtype=jnp.int32)[:, None, None]
    r = jnp.arange(TILE, dtype=jnp.int32)[None, :, None]
    c = jnp.arange(TILE, dtype=jnp.int32)[None, None, :]
    kc = CHUNKS_PER_TILE * (t - 2) + r // CHUNK
    qc = c // CHUNK
    vis = (kc <= qc) & (kc >= qc - BAND_CHUNKS)
    return jnp.where(vis[:, None], tiles, NEG_INF).astype(_F32)


def _moe_and_norm(x1, route, wg, wu, wd, g2, b2):
    n = route.shape[0]
    eid = route[:, 0:2].astype(jnp.int32)
    gate = route[:, 2:4]
    blk_e, buf_tok, buf_dst, buf_gate = _dispatch(eid, gate)
    y2 = _expert_mlp(blk_e, buf_tok, buf_dst, buf_gate, wg, wu, wd, x1, 2 * n + 2 * EXPERT_ROWS)
    return _combine(x1, y2, g2, b2)


def kernel(x_prompt, x_sample, cache_a_k, cache_a_v, cache_a_kidx, cache_b_k, cache_b_v, t5_table, w_in, ln_idx_g,
           ln_idx_b, rel_b, w_o, ln1_g, ln1_b, w_group, w_expert, w_gate, w_up, w_down, ln2_g, ln2_b):
    assert w_in.shape[0] == DEPTH
    B, T, _ = x_prompt.shape
    S, TS, _ = x_sample.shape
    P = cache_a_k.shape[2]
    W = cache_b_k.shape[2]
    assert T % PROJ_ROWS == 0 and (S * TS) % PROJ_ROWS == 0 and P % TILE == 0 and TS <= CHUNK and W == BAND_ROWS

    w = w_in[0]
    cuts = np.cumsum([0, D_HEADS, D_HEADS, D_HEADS, N_HEADS * D_IDX, D_IDX, N_HEADS, D_HEADS, D_HEADS, D_HEADS])
    w_qa, w_ka, w_va, w_qi, w_ki, w_wi, w_qb, w_kb, w_vb = [w[:, cuts[i]:cuts[i + 1]] for i in range(9)]
    wnat = jnp.concatenate([w_ka, w_va, w_kb, w_vb], axis=1).astype(_BF16)
    wsm = jnp.pad(w_ki, ((0, 0), (0, LANES - D_IDX))).astype(_BF16)
    wt = jnp.concatenate([w_qa, w_va, w_qi, w_qb, w_vb], axis=1).T.astype(_BF16)
    wwi = jnp.pad(w_wi.T, ((0, 16 - N_HEADS), (0, 0))).astype(_BF16)
    g_idx = jnp.pad(ln_idx_g[0], (0, LANES - D_IDX)).reshape(1, LANES)
    b_idx = jnp.pad(ln_idx_b[0], (0, LANES - D_IDX)).reshape(1, LANES)
    woa = w_o[0, :D_HEADS].astype(_BF16)
    wob = w_o[0, D_HEADS:].astype(_BF16)
    wr = jnp.pad(jnp.concatenate([w_group[0], w_expert[0]], axis=1),
                 ((0, 0), (0, LANES - N_GROUPS - N_EXPERTS))).astype(_BF16)
    wg, wu, wd = w_gate[0].astype(_BF16), w_up[0].astype(_BF16), w_down[0].astype(_BF16)
    g1, b1 = ln1_g[0].reshape(1, D_MODEL), ln1_b[0].reshape(1, D_MODEL)
    g2, b2 = ln2_g[0].reshape(1, D_MODEL), ln2_b[0].reshape(1, D_MODEL)
    bias_a = _dsa_bias_tiles(t5_table)
    bias_b = _band_bias_tiles(rel_b[0])

    def finish(x_flat, oa, ob):
        x1, route = _post_attn(oa, ob, x_flat, woa, wob, g1, b1, wr)
        return _moe_and_norm(x1, route, wg, wu, wd, g2, b2)

    xp = x_prompt.reshape(B * T, D_MODEL)
    (ka, va, kb, vb, ki, kab, kbb, ki2, qat, vat, qit, qbt, vbt, wit) = _project(xp, wnat, wsm, wt, wwi, g_idx, b_idx)
    tiles = T // TILE
    oa = _dsa(qat, qit, wit, kab, ki2, vat, bias_a, batch=B, q_tiles=tiles, first_block=0, seq_tiles=tiles,
              topk=min(TOPK_MAX, T // 4))
    ob = _band(qbt, kbb, vbt, bias_b, batch=B, q_tiles=tiles, first_block=0, seq_tiles=tiles)
    y_prompt = finish(xp, oa, ob).reshape(B, T, D_MODEL)
    keep = min(BAND_ROWS, T)
    heads = lambda a, b_, t_: a.reshape(1, b_, t_, N_HEADS, HEAD_DIM)
    st_p = (heads(ka, B, T), heads(va, B, T), ki.reshape(1, B, T, D_IDX),
            heads(kb, B, T)[:, :, T - keep:], heads(vb, B, T)[:, :, T - keep:])

    xs = x_sample.reshape(S * TS, D_MODEL)
    (ka, va, kb, vb, ki, kab, kbb, ki2, qat, vat, qit, qbt, vbt, wit) = _project(xs, wnat, wsm, wt, wwi, g_idx, b_idx)
    seq_tiles = P // TILE + 1
    pad_rows = seq_tiles * TILE - P - TS

    def seq_nat(cache, new, width):
        parts = [cache.reshape(S, -1, width).astype(_BF16), new.reshape(S, TS, width).astype(_BF16),
                 jnp.zeros((S, pad_rows, width), _BF16)]
        return jnp.concatenate(parts, axis=1)

    def seq_tr(nat):
        return jnp.transpose(nat.reshape(S, seq_tiles, TILE, -1), (0, 1, 3, 2)).reshape(S * seq_tiles, -1, TILE)

    def q_tiles_of(tr, rows):
        flat = jnp.transpose(tr, (1, 0, 2)).reshape(rows, S, TS)
        return jnp.transpose(jnp.pad(flat, ((0, 0), (0, 0), (0, TILE - TS))), (1, 0, 2))

    cki = cache_a_kidx[0]
    zpad = jnp.zeros(cki.shape[:-1] + (LANES - D_IDX,), cki.dtype)
    cache_ki2 = jnp.concatenate([cki, zpad, zpad, cki], axis=-1)
    k_seq = seq_nat(cache_a_k[0], kab, D_HEADS)
    v_seq = seq_nat(cache_a_v[0], va, D_HEADS)
    ki2_seq = seq_nat(cache_ki2, ki2, 2 * LANES)
    L = P + TS
    oa = _dsa(q_tiles_of(qat, D_HEADS), q_tiles_of(qit, D_HEADS), q_tiles_of(wit, N_HEADS),
              k_seq.reshape(-1, D_HEADS), ki2_seq.reshape(-1, 2 * LANES), seq_tr(v_seq), bias_a,
              batch=S, q_tiles=1, first_block=P // TILE, seq_tiles=seq_tiles, topk=min(TOPK_MAX, L // 4))
    zeros_front = jnp.zeros((S, P - W, D_HEADS), _BF16)
    kb_seq = jnp.concatenate([zeros_front, seq_nat(cache_b_k[0], kbb, D_HEADS)], axis=1)
    vb_seq = jnp.concatenate([zeros_front, seq_nat(cache_b_v[0], vb, D_HEADS)], axis=1)
    ob = _band(q_tiles_of(qbt, D_HEADS), kb_seq.reshape(-1, D_HEADS), seq_tr(vb_seq), bias_b,
               batch=S, q_tiles=1, first_block=P // TILE, seq_tiles=seq_tiles)
    take = lambda o: o.reshape(S, TILE, D_HEADS)[:, :TS].reshape(S * TS, D_HEADS)
    y_sample = finish(xs, take(oa), take(ob)).reshape(S, TS, D_MODEL)
    band_k = jnp.concatenate([cache_b_k[0], heads(kb, S, TS)[0]], axis=1)[:, TS:]
    band_v = jnp.concatenate([cache_b_v[0], heads(vb, S, TS)[0]], axis=1)[:, TS:]
    st_s = (heads(ka, S, TS), heads(va, S, TS), ki.reshape(1, S, TS, D_IDX), band_k[None], band_v[None])

    return (y_prompt, y_sample) + st_p + st_s
```

```python
import functools
import math

import numpy as np
import jax
import jax.numpy as jnp
from jax import lax
from jax.experimental import pallas as pl
from jax.experimental.pallas import tpu as pltpu

D_MODEL = 1024
CHUNK = 64
HEAD_DIM = 64
N_HEADS = 8
D_HEADS = N_HEADS * HEAD_DIM
D_IDX = 64
TOPK_MAX = 256
T5_BUCKETS = 32
T5_MAX_DIST = 128
BAND_CHUNKS = 8
BAND_ROWS = BAND_CHUNKS * CHUNK
REL_CLIP = 128
N_GROUPS = 4
EXPERTS_PER_GROUP = 8
N_EXPERTS = N_GROUPS * EXPERTS_PER_GROUP
D_EXPERT = 512
DEPTH = 1
DEEPNORM_ALPHA = (2 * DEPTH) ** 0.25
LN_EPS = 1e-5
NEG_INF = -1e30
IDX_SCALE = (N_HEADS * D_IDX) ** -0.5
LOG2E = math.log2(math.e)
Q_SCALE = HEAD_DIM ** -0.5 * LOG2E

TILE = 256
CHUNKS_PER_TILE = TILE // CHUNK
LANES = 128
ROW_TILE = D_MODEL // LANES
PROJ_ROWS = 512
EXPERT_ROWS = 256
COUNT_GROUP = 4
RANK_ROWS = 1024
V7X_VMEM_LIMIT = 56 * 1024 * 1024

_BF16 = jnp.bfloat16
_MASK16 = jnp.bfloat16
_F32 = jnp.float32


HALF = 2 ** 15


def _params(sem, vmem=V7X_VMEM_LIMIT):
    return pltpu.CompilerParams(dimension_semantics=sem, vmem_limit_bytes=vmem)


def _to_row_tiles(ref, x):
    rows = x.shape[0]
    for s in range(ROW_TILE):
        ref[pl.ds(s, rows, stride=ROW_TILE), :] = x[:, s * LANES:(s + 1) * LANES]


def _from_row_tiles(ref):
    rows = ref.shape[0] // ROW_TILE
    return jnp.concatenate([ref[pl.ds(s, rows, stride=ROW_TILE), :] for s in range(ROW_TILE)], axis=1)


def _layer_norm_rows(x, g, b):
    mu = jnp.mean(x, axis=-1, keepdims=True)
    xc = x - mu
    var = jnp.mean(xc * xc, axis=-1, keepdims=True)
    return xc * lax.rsqrt(var + LN_EPS) * g + b


def _project_kernel(x_ref, wnat_ref, wsm_ref, wt_ref, wwi_ref, g_ref, b_ref,
                    ka_ref, va_ref, kb_ref, vb_ref, ki_ref,
                    kab_ref, kbb_ref, ki2_ref,
                    qat_ref, vat_ref, qit_ref, qbt_ref, vbt_ref, wit_ref):
    x = x_ref[...]
    xb = x.astype(_BF16)
    xtb = x.T.astype(_BF16)

    def nat(i):
        return jnp.dot(xb, wnat_ref[:, i * D_HEADS:(i + 1) * D_HEADS], preferred_element_type=_F32)

    def put_heads(ref, z):
        for h in range(N_HEADS):
            ref[pl.ds(h, z.shape[0], stride=N_HEADS), :] = z[:, h * HEAD_DIM:(h + 1) * HEAD_DIM]

    ka = nat(0)
    put_heads(ka_ref, ka)
    kab_ref[...] = ka.astype(_BF16)
    put_heads(va_ref, nat(1))
    kb = nat(2)
    put_heads(kb_ref, kb)
    kbb_ref[...] = kb.astype(_BF16)
    put_heads(vb_ref, nat(3))

    zs = jnp.dot(xb, wsm_ref[...], preferred_element_type=_F32)
    lane = lax.broadcasted_iota(jnp.int32, zs.shape, 1)
    mu = jnp.sum(zs, axis=-1, keepdims=True) * (1.0 / D_IDX)
    zc = jnp.where(lane < D_IDX, zs - mu, 0.0)
    var = jnp.sum(zc * zc, axis=-1, keepdims=True) * (1.0 / D_IDX)
    ln = zc * lax.rsqrt(var + LN_EPS) * g_ref[...] + b_ref[...]
    ki_ref[...] = ln[:, :D_IDX]
    ki2_ref[:, :LANES] = ln.astype(_BF16)
    ki2_ref[:, LANES:] = pltpu.roll(ln, D_IDX, axis=1).astype(_BF16)

    def tr(i, scale):
        z = jnp.dot(wt_ref[i * D_HEADS:(i + 1) * D_HEADS, :], xtb, preferred_element_type=_F32)
        if scale != 1.0:
            z = z * scale
        return z.astype(_BF16)

    def put(ref, z):
        for t in range(ref.shape[0]):
            ref[t] = z[:, t * TILE:(t + 1) * TILE]

    put(qat_ref, tr(0, Q_SCALE))
    put(vat_ref, tr(1, 1.0))
    put(qit_ref, tr(2, 1.0))
    put(qbt_ref, tr(3, Q_SCALE))
    put(vbt_ref, tr(4, 1.0))
    wi = jnp.dot(wwi_ref[...], xtb, preferred_element_type=_F32) * IDX_SCALE
    put(wit_ref, wi[:N_HEADS])


def _project(x, wnat, wsm, wt, wwi, g, b):
    n = x.shape[0]
    rows = PROJ_ROWS
    tiles = rows // TILE
    nt = n // TILE
    grid = (n // rows,)
    row_spec = lambda w: pl.BlockSpec((rows, w), lambda i: (i, 0))
    full = lambda a: pl.BlockSpec(a.shape, lambda i: (0,) * a.ndim)
    tspec = lambda h: pl.BlockSpec((tiles, h, TILE), lambda i: (i, 0, 0))
    nat_f32 = jax.ShapeDtypeStruct((n * N_HEADS, HEAD_DIM), _F32)
    nat_b16 = jax.ShapeDtypeStruct((n, D_HEADS), _BF16)
    tr_b16 = jax.ShapeDtypeStruct((nt, D_HEADS, TILE), _BF16)
    out_shape = (nat_f32, nat_f32, nat_f32, nat_f32, jax.ShapeDtypeStruct((n, D_IDX), _F32),
                 nat_b16, nat_b16, jax.ShapeDtypeStruct((n, 2 * LANES), _BF16),
                 tr_b16, tr_b16, tr_b16, tr_b16, tr_b16,
                 jax.ShapeDtypeStruct((nt, N_HEADS, TILE), _F32))
    head_spec = pl.BlockSpec((rows * N_HEADS, HEAD_DIM), lambda i: (i, 0))
    out_specs = (head_spec,) * 4 + (row_spec(D_IDX),) + (row_spec(D_HEADS),) * 2 + (row_spec(2 * LANES),) \
        + (tspec(D_HEADS),) * 5 + (tspec(N_HEADS),)
    return pl.pallas_call(
        _project_kernel, out_shape=out_shape, grid=grid,
        in_specs=[row_spec(D_MODEL), full(wnat), full(wsm), full(wt), full(wwi), full(g), full(b)],
        out_specs=out_specs, name="project",
        compiler_params=_params(("parallel",)),
    )(x, wnat, wsm, wt, wwi, g, b)


def _attend_tile(k_of, qz_ref, add_of, vt_of, s_ref, p_ref, al_ref, m_ref, acc_ref):
    for h in range(N_HEADS):
        s_ref[h] = jnp.dot(k_of(h), qz_ref[h], preferred_element_type=_F32) + add_of(h)
    for h in range(N_HEADS):
        s = s_ref[h]
        m = m_ref[h]
        m_new = jnp.maximum(m, jnp.max(s, axis=0, keepdims=True))
        al_ref[h] = jnp.exp2(m - m_new)
        p_ref[h] = jnp.exp2(s - m_new).astype(_BF16)
        m_ref[h] = m_new
    ones = jnp.ones((16, TILE), _BF16)
    for h in range(N_HEADS):
        lhs = jnp.concatenate([vt_of(h), ones], axis=0)
        acc_ref[h] = al_ref[h] * acc_ref[h] + jnp.dot(lhs, p_ref[h], preferred_element_type=_F32)


def _attend_init(m_ref, acc_ref):
    m_ref[...] = jnp.full(m_ref.shape, NEG_INF, _F32)
    acc_ref[...] = jnp.zeros(acc_ref.shape, _F32)


def _attend_finish(acc_ref, ot_ref, o_ref):
    for h in range(N_HEADS):
        a = acc_ref[h]
        ot_ref[h * HEAD_DIM:(h + 1) * HEAD_DIM, :] = a[:HEAD_DIM] / a[HEAD_DIM:HEAD_DIM + 1]
    o_ref[...] = ot_ref[...].T.astype(_BF16)


def _attend_scratch():
    return [pltpu.VMEM((N_HEADS, TILE, TILE), _F32), pltpu.VMEM((N_HEADS, TILE, TILE), _BF16),
            pltpu.VMEM((N_HEADS, 1, TILE), _F32), pltpu.VMEM((N_HEADS, 1, TILE), _F32),
            pltpu.VMEM((N_HEADS, HEAD_DIM + 16, TILE), _F32)]


def _fill_padded_queries(qt_ref, qz_ref):
    zeros = jnp.zeros((HEAD_DIM, TILE), _BF16)
    for h in range(N_HEADS):
        lo = (h % 2) * HEAD_DIM
        qz_ref[h, lo:lo + HEAD_DIM, :] = qt_ref[0, h * HEAD_DIM:(h + 1) * HEAD_DIM, :]
        qz_ref[h, HEAD_DIM - lo:2 * HEAD_DIM - lo, :] = zeros


def _dsa_kernel(qt_ref, qit_ref, wit_ref, k_ref, ki2_ref, vt_ref, bias_ref, o_ref,
                hi_ref, lo_ref, qz_ref, ot_ref, jstar_ref, s_ref, p_ref, al_ref, m_ref, acc_ref, *, first_block, topk):
    j = pl.program_id(1) + first_block
    n_tiles = j + 1

    _fill_padded_queries(qt_ref, qz_ref)

    row_chunk = lax.broadcasted_iota(jnp.int32, (TILE, TILE), 0) // CHUNK
    col_chunk = lax.broadcasted_iota(jnp.int32, (TILE, TILE), 1) // CHUNK
    diag_adm = row_chunk <= col_chunk

    def rows_of(kt):
        return pl.ds(pl.multiple_of(kt * TILE, TILE), TILE)

    def score_tile(kt, diagonal):
        acc = jnp.zeros((TILE, TILE), _F32)
        for h in range(N_HEADS):
            kslab = ki2_ref[rows_of(kt), (h % 2) * LANES:(h % 2 + 1) * LANES]
            rel = jnp.dot(kslab, qit_ref[0, (h // 2) * LANES:(h // 2 + 1) * LANES, :],
                          preferred_element_type=_F32)
            acc = acc + wit_ref[0, h:h + 1, :] * jnp.maximum(rel, 0.0)
        acc = jnp.where(acc == 0.0, 0.0, acc)
        if diagonal:
            acc = jnp.where(diag_adm, acc, NEG_INF)
        bits = pltpu.bitcast(acc, jnp.int32)
        key = bits ^ ((bits >> 31) & 0x7FFFFFFF)
        hi_ref[rows_of(kt), :] = (key >> 16).astype(jnp.int16)
        lo_ref[rows_of(kt), :] = ((key & 0xFFFF) - HALF).astype(jnp.int16)

    def score_body(kt, carry):
        score_tile(kt, False)
        return carry

    lax.fori_loop(0, j, score_body, 0)
    score_tile(j, True)

    one, zero = jnp.ones((), _MASK16), jnp.zeros((), _MASK16)
    neg = jnp.full((), NEG_INF, _MASK16)

    n_groups = (n_tiles + COUNT_GROUP - 1) // COUNT_GROUP
    lowest = jnp.full((), -HALF, jnp.int16)

    def pad_body(kt, carry):
        hi_ref[rows_of(kt), :] = jnp.full((TILE, TILE), lowest, jnp.int16)
        lo_ref[rows_of(kt), :] = jnp.full((TILE, TILE), lowest, jnp.int16)
        return carry

    lax.fori_loop(n_tiles, n_groups * COUNT_GROUP, pad_body, 0)

    def count(indicator):
        def body(g, cnt):
            parts = []
            for u in range(COUNT_GROUP):
                ind = indicator(g * COUNT_GROUP + u)
                parts += [ind[r * 16:(r + 1) * 16] for r in range(TILE // 16)]
            while len(parts) > 1:
                parts = [parts[i] + parts[i + 1] for i in range(0, len(parts), 2)]
            return cnt + parts[0].astype(_F32)
        cnt16 = lax.fori_loop(0, n_groups, body, jnp.zeros((16, TILE), _F32))
        return jnp.sum(cnt16, axis=0, keepdims=True)

    def to16(v):
        return (v - HALF).astype(jnp.int16)

    def search16(ref, target):
        def bit_body(i, t):
            cand = t | lax.shift_left(jnp.int32(1), 15 - i)
            c16 = to16(cand)
            cnt = count(lambda kt: jnp.where(ref[rows_of(kt), :] >= c16, one, zero))
            return jnp.where(cnt >= target, cand, t)
        return lax.fori_loop(0, 16, bit_body, jnp.zeros((1, TILE), jnp.int32))

    thr_hi = to16(search16(hi_ref, jnp.float32(topk)))
    n_hi_gt = count(lambda kt: jnp.where(hi_ref[rows_of(kt), :] > thr_hi, one, zero))


    def restrict_body(kt, carry):
        lo_ref[rows_of(kt), :] = jnp.where(hi_ref[rows_of(kt), :] == thr_hi, lo_ref[rows_of(kt), :], lowest)
        return carry

    lax.fori_loop(0, n_tiles, restrict_body, 0)
    thr_lo = to16(search16(lo_ref, topk - n_hi_gt))

    def tied(kt, yes, no):
        return jnp.where(hi_ref[rows_of(kt), :] == thr_hi,
                         jnp.where(lo_ref[rows_of(kt), :] == thr_lo, yes, no), no)

    n_gt = n_hi_gt + count(lambda kt: jnp.where(lo_ref[rows_of(kt), :] > thr_lo, one, zero))
    n_eq = count(lambda kt: tied(kt, one, zero))
    need = topk - n_gt

    big = jnp.int32(2 ** 15 - 1)
    jstar_ref[...] = jnp.full((1, TILE), big, jnp.int32)
    excess = jnp.max(jnp.where(n_eq > need, 1, 0)) > 0
    row_iota = lax.broadcasted_iota(jnp.int32, (TILE, TILE), 0).astype(jnp.int16)

    def index_of(kt):
        return row_iota + (kt * TILE).astype(jnp.int16)

    @pl.when(excess)
    def _():
        def idx_body(i, j0):
            cand = j0 | lax.shift_left(jnp.int32(1), 14 - i)
            c16 = cand.astype(jnp.int16)
            cnt = count(lambda kt: tied(kt, jnp.where(index_of(kt) < c16, one, zero), zero))
            return jnp.where(cnt < need, cand, j0)
        j0 = lax.fori_loop(0, 15, idx_body, jnp.zeros((1, TILE), jnp.int32))
        jstar_ref[...] = jnp.where(n_eq > need, j0 + 1, big)

    jstar = jstar_ref[...].astype(jnp.int16)

    def mask_body(kt, carry):
        hi, lo = hi_ref[rows_of(kt), :], lo_ref[rows_of(kt), :]
        tie = jnp.where(lo == thr_lo, jnp.where(index_of(kt) < jstar, zero, neg), neg)
        same_hi = jnp.where(lo > thr_lo, zero, tie)
        mask = jnp.where(hi > thr_hi, zero, jnp.where(hi == thr_hi, same_hi, neg))
        hi_ref[rows_of(kt), :] = pltpu.bitcast(mask, jnp.int16)
        return carry

    lax.fori_loop(0, n_tiles, mask_body, 0)

    _attend_init(m_ref, acc_ref)

    def attend(kt, near):
        mask = pltpu.bitcast(hi_ref[rows_of(kt), :], _MASK16).astype(_F32)
        add_of = (lambda h: mask) if near is None else (lambda h: mask + bias_ref[near, h])
        _attend_tile(lambda h: k_ref[rows_of(kt), (h // 2) * LANES:(h // 2 + 1) * LANES], qz_ref, add_of,
                     lambda h: vt_ref[kt, h * HEAD_DIM:(h + 1) * HEAD_DIM, :], s_ref, p_ref, al_ref, m_ref, acc_ref)

    def far_body(kt, carry):
        attend(kt, None)
        return carry

    lax.fori_loop(0, jnp.maximum(j - 1, 0), far_body, 0)

    @pl.when(j >= 1)
    def _():
        attend(j - 1, 1)

    attend(j, 0)
    _attend_finish(acc_ref, ot_ref, o_ref)


def _dsa(qt, qit, wit, k, ki2, vt, bias, *, batch, q_tiles, first_block, seq_tiles, topk):
    seq = seq_tiles * TILE
    once = pl.Buffered(1)
    qspec = lambda h: pl.BlockSpec((1, h, TILE), lambda b, jj: (b * q_tiles + jj, 0, 0))
    kern = functools.partial(_dsa_kernel, first_block=first_block, topk=topk)
    return pl.pallas_call(
        kern, out_shape=jax.ShapeDtypeStruct((batch * q_tiles * TILE, D_HEADS), _BF16),
        grid=(batch, q_tiles),
        in_specs=[qspec(D_HEADS), qspec(D_HEADS), qspec(N_HEADS),
                  pl.BlockSpec((seq, D_HEADS), lambda b, jj: (b, 0), pipeline_mode=once),
                  pl.BlockSpec((seq, 2 * LANES), lambda b, jj: (b, 0), pipeline_mode=once),
                  pl.BlockSpec((seq_tiles, D_HEADS, TILE), lambda b, jj: (b, 0, 0), pipeline_mode=once),
                  pl.BlockSpec(bias.shape, lambda b, jj: (0, 0, 0, 0), pipeline_mode=once)],
        out_specs=pl.BlockSpec((TILE, D_HEADS), lambda b, jj: (b * q_tiles + jj, 0)),
        scratch_shapes=[pltpu.VMEM((seq + (COUNT_GROUP - 1) * TILE, TILE), jnp.int16)] * 2 + [
                        pltpu.VMEM((N_HEADS, LANES, TILE), _BF16),
                        pltpu.VMEM((D_HEADS, TILE), _F32),
                        pltpu.VMEM((1, TILE), jnp.int32)] + _attend_scratch(),
        name="dsa", compiler_params=_params(("parallel", "arbitrary")),
    )(qt, qit, wit, k, ki2, vt, bias)


def _band_kernel(qt_ref, k0_ref, k1_ref, k2_ref, v0_ref, v1_ref, v2_ref, bias_ref, o_ref,
                 qz_ref, ot_ref, s_ref, p_ref, al_ref, m_ref, acc_ref, *, first_block):
    j = pl.program_id(1) + first_block
    _fill_padded_queries(qt_ref, qz_ref)
    _attend_init(m_ref, acc_ref)
    k_refs = (k0_ref, k1_ref, k2_ref)
    v_refs = (v0_ref, v1_ref, v2_ref)
    for t in range(3):
        off = jnp.where(j - 2 + t >= 0, 0.0, NEG_INF).astype(_F32)
        _attend_tile(lambda h: k_refs[t][:, (h // 2) * LANES:(h // 2 + 1) * LANES], qz_ref,
                     lambda h: bias_ref[t, h] + off,
                     lambda h: v_refs[t][0, h * HEAD_DIM:(h + 1) * HEAD_DIM, :], s_ref, p_ref, al_ref, m_ref, acc_ref)
    _attend_finish(acc_ref, ot_ref, o_ref)


def _band(qt, k, vt, bias, *, batch, q_tiles, first_block, seq_tiles):
    def kspec(t):
        return pl.BlockSpec((TILE, D_HEADS),
                            lambda b, jj: (b * seq_tiles + jnp.maximum(jj + first_block - 2 + t, 0), 0))

    def vspec(t):
        return pl.BlockSpec((1, D_HEADS, TILE),
                            lambda b, jj: (b * seq_tiles + jnp.maximum(jj + first_block - 2 + t, 0), 0, 0))

    kern = functools.partial(_band_kernel, first_block=first_block)
    return pl.pallas_call(
        kern, out_shape=jax.ShapeDtypeStruct((batch * q_tiles * TILE, D_HEADS), _BF16),
        grid=(batch, q_tiles),
        in_specs=[pl.BlockSpec((1, D_HEADS, TILE), lambda b, jj: (b * q_tiles + jj, 0, 0)),
                  kspec(0), kspec(1), kspec(2), vspec(0), vspec(1), vspec(2),
                  pl.BlockSpec(bias.shape, lambda b, jj: (0, 0, 0, 0))],
        out_specs=pl.BlockSpec((TILE, D_HEADS), lambda b, jj: (b * q_tiles + jj, 0)),
        scratch_shapes=[pltpu.VMEM((N_HEADS, LANES, TILE), _BF16), pltpu.VMEM((D_HEADS, TILE), _F32)]
        + _attend_scratch(),
        name="band", compiler_params=_params(("parallel", "arbitrary")),
    )(qt, k, k, k, vt, vt, vt, bias)


def _post_attn_kernel(oa_ref, ob_ref, x_ref, woa_ref, wob_ref, g_ref, b_ref, wr_ref, x1_ref, route_ref):
    h = jnp.dot(oa_ref[...], woa_ref[...], preferred_element_type=_F32)
    h = h + jnp.dot(ob_ref[...], wob_ref[...], preferred_element_type=_F32)
    x1 = _layer_norm_rows(DEEPNORM_ALPHA * x_ref[...] + h, g_ref[...], b_ref[...])
    _to_row_tiles(x1_ref, x1)
    logits = jnp.dot(x1.astype(_BF16), wr_ref[...], preferred_element_type=_F32)
    lane = lax.broadcasted_iota(jnp.int32, logits.shape, 1)
    lane_f = lane.astype(_F32)
    far = jnp.float32(1e9)

    def first_max(mask):
        v = jnp.max(jnp.where(mask, logits, -jnp.inf), axis=-1, keepdims=True)
        i = jnp.min(jnp.where(mask & (logits == v), lane_f, far), axis=-1, keepdims=True)
        return v, i

    gmask = lane < N_GROUPS
    gmax, gsel = first_max(gmask)
    gsum = jnp.sum(jnp.where(gmask, jnp.exp(logits - gmax), 0.0), axis=-1, keepdims=True)
    g_gate = 1.0 / gsum
    lo = N_GROUPS + gsel * EXPERTS_PER_GROUP
    emask = (lane_f >= lo) & (lane_f < lo + EXPERTS_PER_GROUP)
    v1, i1 = first_max(emask)
    v2, i2 = first_max(emask & (lane_f != i1))
    e2 = jnp.exp(v2 - v1)
    den = 1.0 + e2
    gate1 = (1.0 / den) * g_gate
    gate2 = (e2 / den) * g_gate
    route = jnp.where(lane == 0, i1 - N_GROUPS,
                      jnp.where(lane == 1, i2 - N_GROUPS,
                                jnp.where(lane == 2, gate1, jnp.where(lane == 3, gate2, 0.0))))
    route_ref[...] = route


def _post_attn(oa, ob, x, woa, wob, g, b, wr):
    n = x.shape[0]
    rows = PROJ_ROWS
    row_spec = lambda w: pl.BlockSpec((rows, w), lambda i: (i, 0))
    full = lambda a: pl.BlockSpec(a.shape, lambda i: (0,) * a.ndim)
    return pl.pallas_call(
        _post_attn_kernel,
        out_shape=(jax.ShapeDtypeStruct((n * ROW_TILE, LANES), _F32), jax.ShapeDtypeStruct((n, LANES), _F32)),
        grid=(n // rows,),
        in_specs=[row_spec(D_HEADS), row_spec(D_HEADS), row_spec(D_MODEL), full(woa), full(wob), full(g), full(b),
                  full(wr)],
        out_specs=(pl.BlockSpec((rows * ROW_TILE, LANES), lambda i: (i, 0)), row_spec(LANES)),
        name="post_attn", compiler_params=_params(("parallel",)),
    )(oa, ob, x, woa, wob, g, b, wr)


def _expert_kernel(blk_e_ref, tok_ref, tok_next_ref, dst_ref, gate_ref, wg_ref, wu_ref, wd_ref, x_hbm, y_hbm,
                   xbuf, ybuf, gsem, ssem):
    del blk_e_ref
    i = pl.program_id(0)
    nb = pl.num_programs(0)
    slot = i % 2
    rows = EXPERT_ROWS

    def tile_of(r):
        return pl.ds(pl.multiple_of(r * ROW_TILE, ROW_TILE), ROW_TILE)

    def start_gathers(ids_ref, sl):
        def body(r, c):
            pltpu.make_async_copy(x_hbm.at[tile_of(ids_ref[0, 0, r])], xbuf.at[sl, tile_of(r)], gsem.at[sl]).start()
            return c
        lax.fori_loop(0, rows, body, 0, unroll=8)

    def wait_gathers(sl):
        pltpu.make_async_copy(x_hbm.at[pl.ds(0, rows * ROW_TILE)], xbuf.at[sl], gsem.at[sl]).wait()

    def wait_scatters(sl):
        pltpu.make_async_copy(ybuf.at[sl], y_hbm.at[pl.ds(0, rows * ROW_TILE)], ssem.at[sl]).wait()

    @pl.when(i == 0)
    def _():
        start_gathers(tok_ref, 0)

    @pl.when(i + 1 < nb)
    def _():
        start_gathers(tok_next_ref, 1 - slot)

    wait_gathers(slot)

    @pl.when(i >= 2)
    def _():
        wait_scatters(slot)

    xb = jnp.concatenate([xbuf[slot, pl.ds(s, rows, stride=ROW_TILE), :] for s in range(ROW_TILE)],
                         axis=1).astype(_BF16)
    gpre = jnp.dot(xb, wg_ref[0], preferred_element_type=_F32)
    up = jnp.dot(xb, wu_ref[0], preferred_element_type=_F32)
    hmid = (gpre * (1.0 / (1.0 + jnp.exp(-gpre)))) * up
    y = jnp.dot(hmid.astype(_BF16), wd_ref[0], preferred_element_type=_F32) * gate_ref[...]
    for s in range(ROW_TILE):
        ybuf[slot, pl.ds(s, rows, stride=ROW_TILE), :] = y[:, s * LANES:(s + 1) * LANES]

    def scatter_body(r, c):
        pltpu.make_async_copy(ybuf.at[slot, tile_of(r)], y_hbm.at[tile_of(dst_ref[0, 0, r])], ssem.at[slot]).start()
        return c

    lax.fori_loop(0, rows, scatter_body, 0, unroll=8)

    @pl.when(i == nb - 1)
    def _():
        wait_scatters(slot)

        @pl.when(nb >= 2)
        def _():
            wait_scatters(1 - slot)


def _expert_mlp(blk_e, buf_tok, buf_dst, buf_gate, wg, wu, wd, x1t, n_out_rows):
    nb = blk_e.shape[0]
    rows = EXPERT_ROWS
    smem_ids = pl.BlockSpec((1, 1, rows), lambda i, e: (i, 0, 0), memory_space=pltpu.SMEM)
    smem_next = pl.BlockSpec((1, 1, rows), lambda i, e: (jnp.minimum(i + 1, nb - 1), 0, 0), memory_space=pltpu.SMEM)
    wspec = lambda a: pl.BlockSpec((1,) + a.shape[1:], lambda i, e: (e[i], 0, 0))
    buf = pltpu.VMEM((2, rows * ROW_TILE, LANES), _F32)
    grid_spec = pltpu.PrefetchScalarGridSpec(
        num_scalar_prefetch=1, grid=(nb,),
        in_specs=[smem_ids, smem_next, smem_ids, pl.BlockSpec((rows, 1), lambda i, e: (i, 0)),
                  wspec(wg), wspec(wu), wspec(wd), pl.BlockSpec(memory_space=pl.ANY)],
        out_specs=pl.BlockSpec(memory_space=pl.ANY),
        scratch_shapes=[buf, buf, pltpu.SemaphoreType.DMA((2,)), pltpu.SemaphoreType.DMA((2,))])
    ids = buf_tok.reshape(nb, 1, rows)
    return pl.pallas_call(
        _expert_kernel, out_shape=jax.ShapeDtypeStruct((n_out_rows * ROW_TILE, LANES), _F32), grid_spec=grid_spec,
        name="expert_mlp", compiler_params=_params(("arbitrary",)),
    )(blk_e, ids, ids, buf_dst.reshape(nb, 1, rows), buf_gate.reshape(nb * rows, 1), wg, wu, wd, x1t)


def _rank_kernel(route_ref, tri_ref, rank_ref, count_ref, carry_ref):
    @pl.when(pl.program_id(0) == 0)
    def _():
        carry_ref[...] = jnp.zeros(carry_ref.shape, _F32)

    route = route_ref[...]
    lane = lax.broadcasted_iota(jnp.int32, route.shape, 1).astype(_F32)
    oh0 = jnp.where(lane == route[:, 0:1], 1.0, 0.0).astype(_MASK16)
    oh1 = jnp.where(lane == route[:, 1:2], 1.0, 0.0).astype(_MASK16)
    c0 = jnp.dot(tri_ref[...], oh0, preferred_element_type=_F32)
    c1 = jnp.dot(tri_ref[...], oh1, preferred_element_type=_F32)
    base0 = carry_ref[...]
    base1 = base0 + c0[-1:, :]
    r0 = jnp.sum(jnp.where(lane == route[:, 0:1], c0 + base0, 0.0), axis=-1, keepdims=True) - 1.0
    r1 = jnp.sum(jnp.where(lane == route[:, 1:2], c1 + base1, 0.0), axis=-1, keepdims=True) - 1.0
    lane_i = lax.broadcasted_iota(jnp.int32, route.shape, 1)
    rank_ref[...] = jnp.where(lane_i == 0, r0, jnp.where(lane_i == 1, r1, 0.0))
    total = base1 + c1[-1:, :]
    carry_ref[...] = total
    count_ref[...] = total


def _rank(route):
    n = route.shape[0]
    rows = RANK_ROWS if n % RANK_ROWS == 0 else PROJ_ROWS
    tri = jnp.tri(rows, dtype=_MASK16)
    return pl.pallas_call(
        _rank_kernel,
        out_shape=(jax.ShapeDtypeStruct((n, LANES), _F32), jax.ShapeDtypeStruct((1, LANES), _F32)),
        grid=(n // rows,),
        in_specs=[pl.BlockSpec((rows, LANES), lambda i: (i, 0)), pl.BlockSpec((rows, rows), lambda i: (0, 0))],
        out_specs=(pl.BlockSpec((rows, LANES), lambda i: (i, 0)), pl.BlockSpec((1, LANES), lambda i: (0, 0))),
        scratch_shapes=[pltpu.VMEM((1, LANES), _F32)],
        name="rank", compiler_params=_params(("arbitrary",)),
    )(route, tri)


def _dispatch(route):
    n = route.shape[0]
    m = 2 * n
    rows = EXPERT_ROWS
    rank, counts = _rank(route)
    counts = counts[0, :N_EXPERTS].astype(jnp.int32)
    eid = route[:, 0:2].astype(jnp.int32).T.reshape(m)
    rank = rank[:, 0:2].astype(jnp.int32).T.reshape(m)
    gate = route[:, 2:4].T.reshape(m)
    padded = (counts + rows - 1) // rows * rows
    pend = jnp.cumsum(padded)
    pstart = pend - padded
    slot = pstart[eid] + rank
    n_blocks = -(-m // rows) + N_EXPERTS
    n_slots = n_blocks * rows
    ids = jnp.arange(m, dtype=jnp.int32)
    packed = jnp.stack([ids, lax.bitcast_convert_type(gate, jnp.int32)], axis=1)
    buf = jnp.full((n_slots, 2), -1, jnp.int32).at[slot].set(packed)
    buf_id = buf[:, 0]
    is_pad = buf_id < 0
    spare = m + jnp.arange(n_slots, dtype=jnp.int32) % (2 * rows)
    buf_dst = jnp.where(is_pad, spare, buf_id)
    buf_tok = jnp.where(is_pad, 0, jnp.where(buf_id >= n, buf_id - n, buf_id))
    buf_gate = jnp.where(is_pad, 0.0, lax.bitcast_convert_type(buf[:, 1], _F32))
    blk_e = jnp.minimum(jnp.searchsorted(pend, jnp.arange(n_blocks, dtype=jnp.int32) * rows, side='right'),
                        N_EXPERTS - 1).astype(jnp.int32)
    return blk_e, buf_tok, buf_dst, buf_gate


def _combine_kernel(x1_ref, ya_ref, yb_ref, g_ref, b_ref, o_ref):
    moe = _from_row_tiles(ya_ref) + _from_row_tiles(yb_ref)
    o_ref[...] = _layer_norm_rows(DEEPNORM_ALPHA * _from_row_tiles(x1_ref) + moe, g_ref[...], b_ref[...])


def _combine(x1, y2, g, b):
    n = x1.shape[0] // ROW_TILE
    rows = PROJ_ROWS
    steps = n // rows
    full = lambda a: pl.BlockSpec(a.shape, lambda i: (0,) * a.ndim)
    tiles = lambda off: pl.BlockSpec((rows * ROW_TILE, LANES), lambda i: (off + i, 0))
    return pl.pallas_call(
        _combine_kernel, out_shape=jax.ShapeDtypeStruct((n, D_MODEL), _F32), grid=(steps,),
        in_specs=[tiles(0), tiles(0), tiles(steps), full(g), full(b)],
        out_specs=pl.BlockSpec((rows, D_MODEL), lambda i: (i, 0)),
        name="combine", compiler_params=_params(("parallel",)),
    )(x1, y2, y2, g, b)


def _t5_bucket(rel):
    half = T5_BUCKETS // 2
    max_exact = half // 2
    ret = jnp.where(rel > 0, half, 0)
    n = jnp.abs(rel)
    nf = jnp.maximum(n, 1).astype(jnp.float32)
    large = max_exact + (jnp.log(nf / max_exact) / math.log(T5_MAX_DIST / max_exact)
                         * (half - max_exact)).astype(jnp.int32)
    large = jnp.minimum(large, half - 1)
    return ret + jnp.where(n < max_exact, n, large)


def _toeplitz(f):
    period = 2 * TILE
    f = jnp.concatenate([f, jnp.zeros(f.shape[:-1] + (1,), f.dtype)], axis=-1)
    rep = jnp.tile(f, TILE)[..., :TILE * (period - 1)].reshape(f.shape[:-1] + (TILE, period - 1))
    return rep[..., TILE - 1:]


def _dsa_bias_tiles(t5_table):
    span = 2 * TILE - 1
    rel = jnp.arange(3 * TILE - 1, dtype=jnp.int32) - span
    far = t5_table[_t5_bucket(jnp.int32(-2 * TILE))]
    by_rel = ((t5_table[_t5_bucket(rel)] - far) * LOG2E).T
    tiles = [jnp.swapaxes(_toeplitz(by_rel[:, TILE * (1 - d):TILE * (1 - d) + span]), -1, -2) for d in range(2)]
    r = jnp.arange(TILE, dtype=jnp.int32)[:, None]
    c = jnp.arange(TILE, dtype=jnp.int32)[None, :]
    tiles[0] = jnp.where(r // CHUNK <= c // CHUNK, tiles[0], NEG_INF)
    return jnp.stack(tiles).astype(_F32)


def _band_bias_tiles(rel_b):
    span = 2 * TILE - 1
    dist = jnp.arange(4 * TILE - 1, dtype=jnp.int32) - (TILE - 1)
    by_dist = (rel_b[jnp.clip(dist, -REL_CLIP, REL_CLIP) + REL_CLIP] * LOG2E).T
    tiles = jnp.stack([_toeplitz(by_dist[:, TILE * (2 - t):TILE * (2 - t) + span]) for t in range(3)])
    t = jnp.arange(3, dtype=jnp.int32)[:, None, None]
    r = jnp.arange(TILE, dtype=jnp.int32)[None, :, None]
    c = jnp.arange(TILE, dtype=jnp.int32)[None, None, :]
    kc = CHUNKS_PER_TILE * (t - 2) + r // CHUNK
    qc = c // CHUNK
    vis = (kc <= qc) & (kc >= qc - BAND_CHUNKS)
    return jnp.where(vis[:, None], tiles, NEG_INF).astype(_F32)


def _moe_and_norm(x1, route, wg, wu, wd, g2, b2):
    n = route.shape[0]
    blk_e, buf_tok, buf_dst, buf_gate = _dispatch(route)
    y2 = _expert_mlp(blk_e, buf_tok, buf_dst, buf_gate, wg, wu, wd, x1, 2 * n + 2 * EXPERT_ROWS)
    return _combine(x1, y2, g2, b2)


def kernel(x_prompt, x_sample, cache_a_k, cache_a_v, cache_a_kidx, cache_b_k, cache_b_v, t5_table, w_in, ln_idx_g,
           ln_idx_b, rel_b, w_o, ln1_g, ln1_b, w_group, w_expert, w_gate, w_up, w_down, ln2_g, ln2_b):
    assert w_in.shape[0] == DEPTH
    B, T, _ = x_prompt.shape
    S, TS, _ = x_sample.shape
    P = cache_a_k.shape[2]
    W = cache_b_k.shape[2]
    assert T % PROJ_ROWS == 0 and (S * TS) % PROJ_ROWS == 0 and P % TILE == 0 and TS <= CHUNK and W == BAND_ROWS

    w = w_in[0]
    cuts = np.cumsum([0, D_HEADS, D_HEADS, D_HEADS, N_HEADS * D_IDX, D_IDX, N_HEADS, D_HEADS, D_HEADS, D_HEADS])
    w_qa, w_ka, w_va, w_qi, w_ki, w_wi, w_qb, w_kb, w_vb = [w[:, cuts[i]:cuts[i + 1]] for i in range(9)]
    wnat = jnp.concatenate([w_ka, w_va, w_kb, w_vb], axis=1).astype(_BF16)
    wsm = jnp.pad(w_ki, ((0, 0), (0, LANES - D_IDX))).astype(_BF16)
    wt = jnp.concatenate([w_qa, w_va, w_qi, w_qb, w_vb], axis=1).T.astype(_BF16)
    wwi = jnp.pad(w_wi.T, ((0, 16 - N_HEADS), (0, 0))).astype(_BF16)
    g_idx = jnp.pad(ln_idx_g[0], (0, LANES - D_IDX)).reshape(1, LANES)
    b_idx = jnp.pad(ln_idx_b[0], (0, LANES - D_IDX)).reshape(1, LANES)
    woa = w_o[0, :D_HEADS].astype(_BF16)
    wob = w_o[0, D_HEADS:].astype(_BF16)
    wr = jnp.pad(jnp.concatenate([w_group[0], w_expert[0]], axis=1),
                 ((0, 0), (0, LANES - N_GROUPS - N_EXPERTS))).astype(_BF16)
    wg, wu, wd = w_gate[0].astype(_BF16), w_up[0].astype(_BF16), w_down[0].astype(_BF16)
    g1, b1 = ln1_g[0].reshape(1, D_MODEL), ln1_b[0].reshape(1, D_MODEL)
    g2, b2 = ln2_g[0].reshape(1, D_MODEL), ln2_b[0].reshape(1, D_MODEL)
    bias_a = _dsa_bias_tiles(t5_table)
    bias_b = _band_bias_tiles(rel_b[0])

    def finish(x_flat, oa, ob):
        x1, route = _post_attn(oa, ob, x_flat, woa, wob, g1, b1, wr)
        return _moe_and_norm(x1, route, wg, wu, wd, g2, b2)

    xp = x_prompt.reshape(B * T, D_MODEL)
    (ka, va, kb, vb, ki, kab, kbb, ki2, qat, vat, qit, qbt, vbt, wit) = _project(xp, wnat, wsm, wt, wwi, g_idx, b_idx)
    tiles = T // TILE
    oa = _dsa(qat, qit, wit, kab, ki2, vat, bias_a, batch=B, q_tiles=tiles, first_block=0, seq_tiles=tiles,
              topk=min(TOPK_MAX, T // 4))
    ob = _band(qbt, kbb, vbt, bias_b, batch=B, q_tiles=tiles, first_block=0, seq_tiles=tiles)
    y_prompt = finish(xp, oa, ob).reshape(B, T, D_MODEL)
    keep = min(BAND_ROWS, T)
    heads = lambda a, b_, t_: a.reshape(1, b_, t_, N_HEADS, HEAD_DIM)
    st_p = (heads(ka, B, T), heads(va, B, T), ki.reshape(1, B, T, D_IDX),
            heads(kb, B, T)[:, :, T - keep:], heads(vb, B, T)[:, :, T - keep:])

    xs = x_sample.reshape(S * TS, D_MODEL)
    (ka, va, kb, vb, ki, kab, kbb, ki2, qat, vat, qit, qbt, vbt, wit) = _project(xs, wnat, wsm, wt, wwi, g_idx, b_idx)
    seq_tiles = P // TILE + 1
    pad_rows = seq_tiles * TILE - P - TS

    def seq_nat(cache, new, width):
        parts = [cache.reshape(S, -1, width).astype(_BF16), new.reshape(S, TS, width).astype(_BF16),
                 jnp.zeros((S, pad_rows, width), _BF16)]
        return jnp.concatenate(parts, axis=1)

    def seq_tr(nat):
        return jnp.transpose(nat.reshape(S, seq_tiles, TILE, -1), (0, 1, 3, 2)).reshape(S * seq_tiles, -1, TILE)

    def q_tiles_of(tr, rows):
        flat = jnp.transpose(tr, (1, 0, 2)).reshape(rows, S, TS)
        return jnp.transpose(jnp.pad(flat, ((0, 0), (0, 0), (0, TILE - TS))), (1, 0, 2))

    cki = cache_a_kidx[0]
    zpad = jnp.zeros(cki.shape[:-1] + (LANES - D_IDX,), cki.dtype)
    cache_ki2 = jnp.concatenate([cki, zpad, zpad, cki], axis=-1)
    k_seq = seq_nat(cache_a_k[0], kab, D_HEADS)
    v_seq = seq_nat(cache_a_v[0], va, D_HEADS)
    ki2_seq = seq_nat(cache_ki2, ki2, 2 * LANES)
    L = P + TS
    oa = _dsa(q_tiles_of(qat, D_HEADS), q_tiles_of(qit, D_HEADS), q_tiles_of(wit, N_HEADS),
              k_seq.reshape(-1, D_HEADS), ki2_seq.reshape(-1, 2 * LANES), seq_tr(v_seq), bias_a,
              batch=S, q_tiles=1, first_block=P // TILE, seq_tiles=seq_tiles, topk=min(TOPK_MAX, L // 4))
    zeros_front = jnp.zeros((S, P - W, D_HEADS), _BF16)
    kb_seq = jnp.concatenate([zeros_front, seq_nat(cache_b_k[0], kbb, D_HEADS)], axis=1)
    vb_seq = jnp.concatenate([zeros_front, seq_nat(cache_b_v[0], vb, D_HEADS)], axis=1)
    ob = _band(q_tiles_of(qbt, D_HEADS), kb_seq.reshape(-1, D_HEADS), seq_tr(vb_seq), bias_b,
               batch=S, q_tiles=1, first_block=P // TILE, seq_tiles=seq_tiles)
    take = lambda o: o.reshape(S, TILE, D_HEADS)[:, :TS].reshape(S * TS, D_HEADS)
    y_sample = finish(xs, take(oa), take(ob)).reshape(S, TS, D_MODEL)
    band_k = jnp.concatenate([cache_b_k[0], heads(kb, S, TS)[0]], axis=1)[:, TS:]
    band_v = jnp.concatenate([cache_b_v[0], heads(vb, S, TS)[0]], axis=1)[:, TS:]
    st_s = (heads(ka, S, TS), heads(va, S, TS), ki.reshape(1, S, TS, D_IDX), band_k[None], band_v[None])

    return (y_prompt, y_sample) + st_p + st_s
```
